```python
import math
import jax, jax.numpy as jnp
from jax import lax
import numpy as np


D_MODEL = 1024
BATCH = 4
SEQ = 4096
DEPTH = 1
DEC_BATCH = 32
DEC_SEQ = 8
PAST_LEN = 8192
PAGE_SIZE = 128

H_RET = 8
DK_RET = 64
DV_RET = 64
D_RET = H_RET * DV_RET
RET_CHUNK = 128
H_ATT = 8
HD_ATT = 64
D_ATT = H_ATT * HD_ATT
BRANCHES = ((128, 1), (512, 4), (2048, 16))
MAX_WINDOW = 2048
Q_BLOCK = 128
D_MIX = D_RET + D_ATT
D_IN = 2 * H_RET * DK_RET + 2 * D_RET + 3 * D_ATT
D_FF = 2816
CONV_W = 3
EPS = 1e-6

kernel_name = 'hybrid_retention_dilated_attn_convffn_step'


def _rmsnorm(x, w):
    xf = x.astype(jnp.float32)
    y = xf * lax.rsqrt(jnp.mean(xf * xf, axis=-1, keepdims=True) + EPS)
    return (y * w.astype(jnp.float32)).astype(x.dtype)


def _retention(q, k, v, s0):
    B, T = q.shape[0], q.shape[1]
    C = math.gcd(T, RET_CHUNK)
    nc = T // C
    lg = jnp.log1p(-jnp.exp2(-5.0 - jnp.arange(H_RET, dtype=jnp.float32)))
    i = jnp.arange(C, dtype=jnp.float32)
    diff = i[:, None] - i[None, :]
    intra = jnp.where(diff >= 0, jnp.exp(lg[:, None, None] * jnp.maximum(diff, 0.0)), 0.0)
    q_dec = jnp.exp(lg[None, :] * (i[:, None] + 1.0))[None, :, :, None]
    k_dec = jnp.exp(lg[None, :] * (C - 1.0 - i[:, None]))[None, :, :, None]
    c_dec = jnp.exp(lg * C)[None, :, None, None]

    def to_chunks(a):
        return a.astype(jnp.float32).reshape(B, nc, C, H_RET, a.shape[-1]).swapaxes(0, 1)

    qc = to_chunks(q)
    kc = to_chunks(k) * (DK_RET ** -0.5)
    vc = to_chunks(v)

    def step(S, inp):
        qi, ki, vi = inp
        sc = jnp.einsum('bihd,bjhd->bhij', qi, ki) * intra
        o = jnp.einsum('bhij,bjhe->bihe', sc, vi) + jnp.einsum('bihd,bhde->bihe', qi, S) * q_dec
        S = S * c_dec + jnp.einsum('bjhd,bjhe->bhde', ki * k_dec, vi)
        return S, o

    S, o = lax.scan(step, s0.astype(jnp.float32), (qc, kc, vc))
    o = o.swapaxes(0, 1).reshape(B, T, H_RET, DV_RET)
    return o, S


def _dilated_attention(q, q_pos, k_all, v_all, k_start):
    slopes = jnp.exp2(-8.0 * jnp.arange(1, H_ATT + 1, dtype=jnp.float32) / H_ATT)
    qf = q.astype(jnp.float32) * (HD_ATT ** -0.5)
    ms, dens, nums = [], [], []
    for window, dil in BRANCHES:
        dist = dil * jnp.arange(window // dil + 1, dtype=jnp.int32)
        local = q_pos[:, None] - dist[None, :] - k_start
        valid = local >= 0
        idx = jnp.maximum(local, 0)
        kg = k_all[:, idx].astype(jnp.float32)
        vg = v_all[:, idx].astype(jnp.float32)
        s = jnp.einsum('bqhd,bqnhd->bhqn', qf, kg) - slopes[:, None, None] * dist.astype(jnp.float32)
        s = jnp.where(valid, s, -jnp.inf)
        m = jnp.max(s, axis=-1)
        p = jnp.exp(s - m[..., None])
        ms.append(m)
        dens.append(jnp.sum(p, axis=-1))
        nums.append(jnp.einsum('bhqn,bqnhd->bqhd', p, vg))
    M = ms[0]
    for m in ms[1:]:
        M = jnp.maximum(M, m)
    num = 0.0
    den = 0.0
    for m, dn, nm in zip(ms, dens, nums):
        sc = jnp.exp(m - M)
        num = num + nm * sc.transpose(0, 2, 1)[..., None]
        den = den + dn * sc
    return num / den.transpose(0, 2, 1)[..., None]


def _layer(x, ret_s0, win_k, win_v, conv_s0, pos0, norm1_w, w_in, ret_gn_w, ret_gn_b, w_out,
           norm2_w, w_up, conv_w, conv_b, w_down):
    B, T = x.shape[0], x.shape[1]
    h = _rmsnorm(x, norm1_w)
    proj = h @ w_in
    sizes = (H_RET * DK_RET, H_RET * DK_RET, D_RET, D_RET, D_ATT, D_ATT, D_ATT)
    points = [sum(sizes[:j]) for j in range(1, len(sizes))]
    rq, rk, rv, rg, aq, ak, av = jnp.split(proj, points, axis=-1)

    ro, ret_s = _retention(rq.reshape(B, T, H_RET, DK_RET), rk.reshape(B, T, H_RET, DK_RET),
                           rv.reshape(B, T, H_RET, DV_RET), ret_s0)
    mu = jnp.mean(ro, axis=-1, keepdims=True)
    var = jnp.mean(jnp.square(ro - mu), axis=-1, keepdims=True)
    ro = ((ro - mu) * lax.rsqrt(var + EPS)).reshape(B, T, D_RET)
    ro = ro * ret_gn_w.astype(jnp.float32) + ret_gn_b.astype(jnp.float32)
    ro = jax.nn.silu(rg.astype(jnp.float32)) * ro

    ak = ak.reshape(B, T, H_ATT, HD_ATT)
    av = av.reshape(B, T, H_ATT, HD_ATT)
    k_all = jnp.concatenate([win_k.astype(ak.dtype), ak], axis=1)
    v_all = jnp.concatenate([win_v.astype(av.dtype), av], axis=1)
    k_start = pos0 - win_k.shape[1]
    qb = math.gcd(T, Q_BLOCK)
    nb = T // qb
    q_blocks = aq.reshape(B, nb, qb, H_ATT, HD_ATT).swapaxes(0, 1)
    pos_blocks = (pos0 + jnp.arange(T, dtype=jnp.int32)).reshape(nb, qb)
    ao = lax.map(lambda blk: _dilated_attention(blk[0], blk[1], k_all, v_all, k_start),
                 (q_blocks, pos_blocks))
    ao = ao.swapaxes(0, 1).reshape(B, T, D_ATT)
    keep = min(MAX_WINDOW, k_all.shape[1])

    x = x + jnp.concatenate([ro, ao], axis=-1).astype(x.dtype) @ w_out

    h = _rmsnorm(x, norm2_w)
    ua, ub = jnp.split(h @ w_up, [D_FF], axis=-1)
    buf = jnp.concatenate([conv_s0.astype(ua.dtype), ua], axis=1)
    conv = conv_b
    for j in range(CONV_W):
        conv = conv + buf[:, j:j + T] * conv_w[j]
    x = x + (jax.nn.silu(conv) * ub) @ w_down
    return (x, ret_s.astype(ret_s0.dtype), k_all[:, -keep:], v_all[:, -keep:], buf[:, -(CONV_W - 1):])


def setup_inputs(seed: int = 0) -> dict:
    key = jax.random.key(seed)
    ks = jax.random.split(key, 20)
    win_rows = min(MAX_WINDOW, PAST_LEN)
    nrm = jax.random.normal
    f32 = jnp.float32
    return {
        'x_prompt': nrm(ks[0], (BATCH, SEQ, D_MODEL), f32),
        'x_sample': nrm(ks[1], (DEC_BATCH, DEC_SEQ, D_MODEL), f32),
        'state_ret': 0.5 * nrm(ks[2], (DEC_BATCH, H_RET, DK_RET, DV_RET), f32),
        'cache_win_k': nrm(ks[3], (DEC_BATCH, win_rows, H_ATT, HD_ATT), f32),
        'cache_win_v': nrm(ks[4], (DEC_BATCH, win_rows, H_ATT, HD_ATT), f32),
        'state_conv': nrm(ks[5], (DEC_BATCH, CONV_W - 1, D_FF), f32),
        'norm1_w': 1.0 + 0.02 * nrm(ks[6], (D_MODEL,), f32),
        'w_in': nrm(ks[7], (D_MODEL, D_IN), f32) * D_MODEL ** -0.5,
        'ret_gn_w': 1.0 + 0.02 * nrm(ks[8], (D_RET,), f32),
        'ret_gn_b': 0.02 * nrm(ks[9], (D_RET,), f32),
        'w_out': nrm(ks[10], (D_MIX, D_MODEL), f32) * D_MIX ** -0.5,
        'norm2_w': 1.0 + 0.02 * nrm(ks[11], (D_MODEL,), f32),
        'w_up': nrm(ks[12], (D_MODEL, 2 * D_FF), f32) * D_MODEL ** -0.5,
        'conv_w': nrm(ks[13], (CONV_W, D_FF), f32) * CONV_W ** -0.5,
        'conv_b': 0.02 * nrm(ks[14], (D_FF,), f32),
        'w_down': nrm(ks[15], (D_FF, D_MODEL), f32) * D_FF ** -0.5,
        'normf_w': 1.0 + 0.02 * nrm(ks[16], (D_MODEL,), f32),
    }


def reference(x_prompt, x_sample, state_ret, cache_win_k, cache_win_v, state_conv, norm1_w, w_in,
              ret_gn_w, ret_gn_b, w_out, norm2_w, w_up, conv_w, conv_b, w_down, normf_w):
    weights = (norm1_w, w_in, ret_gn_w, ret_gn_b, w_out, norm2_w, w_up, conv_w, conv_b, w_down)
    b = x_prompt.shape[0]
    zero_ret = jnp.zeros((b, H_RET, DK_RET, DV_RET), x_prompt.dtype)
    empty_kv = jnp.zeros((b, 0, H_ATT, HD_ATT), x_prompt.dtype)
    zero_conv = jnp.zeros((b, CONV_W - 1, D_FF), x_prompt.dtype)
    hp, hs = x_prompt, x_sample
    for _ in range(DEPTH):
        hp, ret_p, wk_p, wv_p, conv_p = _layer(hp, zero_ret, empty_kv, empty_kv, zero_conv, 0, *weights)
        hs, ret_s, wk_s, wv_s, conv_s = _layer(hs, state_ret, cache_win_k, cache_win_v, state_conv,
                                               PAST_LEN, *weights)
    y_prompt = _rmsnorm(hp, normf_w)
    y_sample = _rmsnorm(hs, normf_w)
    return (y_prompt, y_sample, ret_p, ret_s, wk_p, wv_p, wk_s, wv_s, conv_p, conv_s)
```

```python
import functools
import math

import jax
import jax.numpy as jnp
from jax import lax
from jax.experimental import pallas as pl
from jax.experimental.pallas import tpu as pltpu

F32 = jnp.float32
BF16 = jnp.bfloat16

D_MODEL = 1024
N_HEADS = 8
HEAD_DIM = 64
D_HEADS = N_HEADS * HEAD_DIM
N_GROUPS = 7
D_IN = N_GROUPS * D_HEADS
PAIR = 2 * HEAD_DIM
N_PAIRS = N_HEADS // 2
RET_CHUNK = 128
BRANCHES = ((128, 1), (512, 4), (2048, 16))
MAX_WINDOW = 2048
Q_TILE = 128
D_FF = 2816
CONV_W = 3
EPS = 1e-6
NEG_INF = float("-inf")

V7X_VMEM_LIMIT = 56 * 1024 * 1024


def _params(sem, vmem=V7X_VMEM_LIMIT):
    return pltpu.CompilerParams(dimension_semantics=sem, vmem_limit_bytes=vmem)


def _const_spec(shape):
    nd = len(shape)
    return pl.BlockSpec(shape, lambda *_: (0,) * nd, pipeline_mode=pl.Buffered(1))


def _rmsnorm(x, w):
    return x * lax.rsqrt(jnp.mean(x * x, axis=-1, keepdims=True) + EPS) * w


def _silu(x):
    return x * jax.nn.sigmoid(x)


def _dot(a, b):
    return jnp.dot(a, b, preferred_element_type=F32)


def _dot_nt(a, b):
    return lax.dot_general(a, b, (((1,), (1,)), ((), ())), preferred_element_type=F32)


def _dot_tn(a, b):
    return lax.dot_general(a, b, (((0,), (0,)), ((), ())), preferred_element_type=F32)


def _inproj_kernel(x_ref, nw_ref, w_ref, proj_ref, *win_refs, win_first_tile, tiles_per_seq):
    x = x_ref[...]
    h = _rmsnorm(x, nw_ref[...]).astype(BF16)
    for g in range(N_GROUPS):
        cols = slice(g * D_HEADS, (g + 1) * D_HEADS)
        r = _dot(h, w_ref[:, cols])
        if g in (1, 4):
            r = r * (HEAD_DIM ** -0.5)
        proj_ref[:, cols] = r.astype(proj_ref.dtype)
        if win_refs and g in (5, 6):
            out = win_refs[g - 5]

            @pl.when(pl.program_id(0) % tiles_per_seq >= win_first_tile)
            def _():
                out[...] = r


def _inproj(x2d, norm_w, w_bf16, *, tile, out_dtype, seq_len=None, win_rows=None):
    n = x2d.shape[0]
    assert n % tile == 0
    in_specs = [
        pl.BlockSpec((tile, D_MODEL), lambda i: (i, 0)),
        _const_spec((1, D_MODEL)),
        _const_spec((D_MODEL, D_IN)),
    ]
    out_shape = [jax.ShapeDtypeStruct((n, D_IN), out_dtype)]
    out_specs = [pl.BlockSpec((tile, D_IN), lambda i: (i, 0))]
    kw = dict(win_first_tile=0, tiles_per_seq=1)
    if win_rows is not None:
        assert seq_len % tile == 0 and win_rows % tile == 0
        tps = seq_len // tile
        first = (seq_len - win_rows) // tile
        wps = win_rows // tile
        kw = dict(win_first_tile=first, tiles_per_seq=tps)

        def win_map(i):
            return ((i // tps) * wps + jnp.maximum(i % tps - first, 0), 0)

        for _ in range(2):
            out_shape.append(jax.ShapeDtypeStruct((n // seq_len * win_rows, D_HEADS), F32))
            out_specs.append(pl.BlockSpec((tile, D_HEADS), win_map))
    return pl.pallas_call(
        functools.partial(_inproj_kernel, **kw),
        grid=(n // tile,),
        in_specs=in_specs,
        out_specs=out_specs,
        out_shape=out_shape,
        compiler_params=_params(("arbitrary",)),
        name="inproj",
    )(x2d, norm_w.reshape(1, D_MODEL), w_bf16)


def _retention_tables(chunk):
    lg = jnp.log1p(-jnp.exp2(-5.0 - jnp.arange(N_HEADS, dtype=F32)))
    i = jnp.arange(chunk, dtype=F32)
    diff = i[:, None] - i[None, :]
    intra = jnp.where(diff >= 0, jnp.exp(lg[:, None, None] * jnp.maximum(diff, 0.0)), 0.0)
    lane_head = jnp.arange(D_HEADS) // HEAD_DIM
    q_dec = jnp.exp(lg[lane_head][None, :] * (i[:, None] + 1.0))
    k_dec = jnp.exp(lg[lane_head][None, :] * (chunk - 1.0 - i[:, None]))
    c_dec = jnp.exp(lg * chunk)
    row_head = jnp.arange(PAIR) // HEAD_DIM
    same = (row_head[:, None] == row_head[None, :]).astype(F32)
    c_pair = c_dec.reshape(N_PAIRS, 2)[:, row_head][:, :, None] * same[None]
    return intra, q_dec, k_dec, c_pair, same


def _retention_kernel(proj_ref, s0_ref, intra_ref, qdec_ref, kdec_ref, cpair_ref, same_ref, gw_ref, gb_ref,
                      ro_ref, sout_ref, state_ref, *, chunk, chunks_per_tile):
    t = pl.program_id(1)

    @pl.when(t == 0)
    def _():
        state_ref[...] = s0_ref[0]

    left = lax.broadcasted_iota(jnp.int32, (chunk, PAIR), 1) < HEAD_DIM
    for c in range(chunks_per_tile):
        rows = slice(c * chunk, (c + 1) * chunk)
        for p in range(N_PAIRS):
            lanes = slice(p * PAIR, (p + 1) * PAIR)

            def grp(g):
                return proj_ref[0, rows, g * D_HEADS + p * PAIR:g * D_HEADS + (p + 1) * PAIR]

            q2 = grp(0).astype(BF16)
            k2 = grp(1).astype(BF16)
            v2 = grp(2).astype(BF16)
            gate = grp(3).astype(F32)
            s_pair = state_ref[p]
            o = None
            for hh in range(2):
                mine = left if hh == 0 else jnp.logical_not(left)
                qm = jnp.where(mine, q2, jnp.zeros_like(q2))
                sc = _dot_nt(qm, k2) * intra_ref[2 * p + hh]
                oh = _dot(sc.astype(BF16), v2)
                o = oh if o is None else jnp.where(left, o, oh)
            o = o + _dot(q2, s_pair.astype(BF16)) * qdec_ref[:, lanes]
            kd = (k2.astype(F32) * kdec_ref[:, lanes]).astype(BF16)
            state_ref[p] = s_pair * cpair_ref[p] + _dot_tn(kd, v2) * same_ref[...]

            def head_mean(a):
                zero = jnp.zeros_like(a)
                sl = jnp.sum(jnp.where(left, a, zero), axis=-1, keepdims=True)
                sr = jnp.sum(jnp.where(left, zero, a), axis=-1, keepdims=True)
                return jnp.where(left, sl, sr) * (1.0 / HEAD_DIM)

            dlt = o - head_mean(o)
            var = head_mean(dlt * dlt)
            y = dlt * lax.rsqrt(var + EPS) * gw_ref[:, lanes] + gb_ref[:, lanes]
            ro_ref[0, rows, lanes] = (_silu(gate) * y).astype(ro_ref.dtype)

    @pl.when(t == pl.num_programs(1) - 1)
    def _():
        sout_ref[0] = state_ref[...]


def _pair_states(s):
    b = s.shape[0]
    s = s.astype(F32).reshape(b, N_PAIRS, 2, HEAD_DIM, HEAD_DIM)
    z = jnp.zeros_like(s[:, :, 0])
    top = jnp.concatenate([s[:, :, 0], z], axis=-1)
    bot = jnp.concatenate([z, s[:, :, 1]], axis=-1)
    return jnp.concatenate([top, bot], axis=-2)


def _unpair_states(sp):
    a = sp[:, :, :HEAD_DIM, :HEAD_DIM]
    d = sp[:, :, HEAD_DIM:, HEAD_DIM:]
    b = sp.shape[0]
    return jnp.stack([a, d], axis=2).reshape(b, N_HEADS, HEAD_DIM, HEAD_DIM)


def _retention(proj, s0, gn_w, gb_b, *, tile):
    b, t, _ = proj.shape
    chunk = math.gcd(t, RET_CHUNK)
    assert t % tile == 0 and tile % chunk == 0
    intra, q_dec, k_dec, c_pair, same = _retention_tables(chunk)
    ro, s_out = pl.pallas_call(
        functools.partial(_retention_kernel, chunk=chunk, chunks_per_tile=tile // chunk),
        grid=(b, t // tile),
        in_specs=[
            pl.BlockSpec((1, tile, 4 * D_HEADS), lambda i, j: (i, j, 0)),
            pl.BlockSpec((1, N_PAIRS, PAIR, PAIR), lambda i, j: (i, 0, 0, 0)),
            _const_spec((N_HEADS, chunk, chunk)),
            _const_spec((chunk, D_HEADS)),
            _const_spec((chunk, D_HEADS)),
            _const_spec((N_PAIRS, PAIR, PAIR)),
            _const_spec((PAIR, PAIR)),
            _const_spec((1, D_HEADS)),
            _const_spec((1, D_HEADS)),
        ],
        out_specs=[
            pl.BlockSpec((1, tile, D_HEADS), lambda i, j: (i, j, 0)),
            pl.BlockSpec((1, N_PAIRS, PAIR, PAIR), lambda i, j: (i, 0, 0, 0)),
        ],
        out_shape=[
            jax.ShapeDtypeStruct((b, t, D_HEADS), BF16),
            jax.ShapeDtypeStruct((b, N_PAIRS, PAIR, PAIR), F32),
        ],
        scratch_shapes=[pltpu.VMEM((N_PAIRS, PAIR, PAIR), F32)],
        compiler_params=_params(("arbitrary", "arbitrary")),
        name="retention",
    )(proj, _pair_states(s0), intra, q_dec, k_dec, c_pair, same,
      gn_w.reshape(1, D_HEADS).astype(F32), gb_b.reshape(1, D_HEADS).astype(F32))
    return ro, _unpair_states(s_out)


def _alibi_slopes():
    return jnp.exp2(-8.0 * jnp.arange(1, N_HEADS + 1, dtype=F32) / N_HEADS)


def _prompt_attn_kernel(q_ref, k_ref, v_ref, slope_ref, o_ref, xf_ref, qp_ref, kp_ref, vp_ref, st_ref, *, seq):
    hp = pl.program_id(1)
    left = lax.broadcasted_iota(jnp.int32, (Q_TILE, PAIR), 1) < HEAD_DIM
    ones = jnp.ones((2 * Q_TILE, PAIR), BF16)

    xf_ref[0] = q_ref[0].astype(F32)
    xf_ref[1] = k_ref[0].astype(F32)
    xf_ref[2] = v_ref[0].astype(F32)

    def tile(q, k, v, biases):
        nk = k.shape[0]
        vaug = jnp.concatenate([v, ones[:nk]], axis=1)
        res = []
        for hh in range(2):
            mine = left if hh == 0 else jnp.logical_not(left)
            qm = jnp.where(mine, q, jnp.zeros_like(q))
            s = _dot_nt(qm, k) + biases[hh]
            m = jnp.max(s, axis=-1, keepdims=True)
            p = jnp.exp(s - m).astype(BF16)
            pv = _dot(p, vaug)
            res.append((jnp.broadcast_to(m, (Q_TILE, PAIR)), pv[:, PAIR:], pv[:, :PAIR]))
        return tuple(jnp.where(left, a, b) for a, b in zip(res[0], res[1]))

    for bi, (window, d) in enumerate(BRANCHES):
        assert window // d == Q_TILE
        seg = seq // d
        tiles_per_seg = seg // Q_TILE
        qi = lax.broadcasted_iota(jnp.int32, (Q_TILE, 2 * Q_TILE), 0)
        kj = lax.broadcasted_iota(jnp.int32, (Q_TILE, 2 * Q_TILE), 1)
        n_rest = qi + Q_TILE - kj
        n_first = n_rest[:, :Q_TILE] - Q_TILE
        bias_rest, bias_first = [], []
        for hh in range(2):
            sl = slope_ref[pl.ds(2 * hp + hh, 1), :]
            sl2 = jnp.concatenate([sl, sl], axis=1)
            pen_rest = sl2 * (d * n_rest).astype(F32)
            pen_first = sl * (d * n_first).astype(F32)
            bias_rest.append(jnp.where((n_rest >= 0) & (n_rest <= Q_TILE), -pen_rest, NEG_INF))
            bias_first.append(jnp.where(n_first >= 0, -pen_first, NEG_INF))

        if d == 1:
            qs, ks, vs = q_ref.at[0], k_ref.at[0], v_ref.at[0]
        else:
            qs, ks, vs = qp_ref, kp_ref, vp_ref
            for r in range(d):
                dst = slice(r * seg, (r + 1) * seg)
                qp_ref[dst, :] = xf_ref[0, pl.ds(r, seg, stride=d), :].astype(BF16)
                kp_ref[dst, :] = xf_ref[1, pl.ds(r, seg, stride=d), :].astype(BF16)
                vp_ref[dst, :] = xf_ref[2, pl.ds(r, seg, stride=d), :].astype(BF16)

        def merge(r, tile_in_seg, stats, *, d=d, first_branch=(bi == 0)):
            m, den, num = stats
            start = d * Q_TILE * tile_in_seg + r
            idx = pl.ds(start, Q_TILE) if d == 1 else pl.ds(start, Q_TILE, stride=d)
            if first_branch:
                st_ref[0, idx, :] = m
                st_ref[1, idx, :] = den
                st_ref[2, idx, :] = num
            else:
                m0 = st_ref[0, idx, :]
                mm = jnp.maximum(m0, m)
                w0 = jnp.exp(m0 - mm)
                w1 = jnp.exp(m - mm)
                st_ref[0, idx, :] = mm
                st_ref[1, idx, :] = st_ref[1, idx, :] * w0 + den * w1
                st_ref[2, idx, :] = st_ref[2, idx, :] * w0 + num * w1

        def seg_body(r, carry, *, seg=seg, tiles_per_seg=tiles_per_seg, qs=qs, ks=ks, vs=vs,
                     bias_first=bias_first, bias_rest=bias_rest, merge=merge):
            base = pl.multiple_of(r * seg, Q_TILE)
            rows0 = pl.ds(base, Q_TILE)
            merge(r, 0, tile(qs[rows0, :], ks[rows0, :], vs[rows0, :], bias_first))

            def tile_body(j, carry2):
                q0 = pl.multiple_of(base + j * Q_TILE, Q_TILE)
                kv_rows = pl.ds(q0 - Q_TILE, 2 * Q_TILE)
                merge(r, j, tile(qs[pl.ds(q0, Q_TILE), :], ks[kv_rows, :], vs[kv_rows, :], bias_rest))
                return carry2

            return lax.fori_loop(1, tiles_per_seg, tile_body, carry)

        lax.fori_loop(0, d, seg_body, 0)

    o_ref[0] = (st_ref[2] / st_ref[1]).astype(o_ref.dtype)


def _prompt_attn(proj):
    b, t, _ = proj.shape
    assert t % (BRANCHES[-1][1] * Q_TILE) == 0
    slopes = jnp.broadcast_to(_alibi_slopes()[:, None], (N_HEADS, PAIR))

    def col_spec(group):
        return pl.BlockSpec((1, t, PAIR), lambda i, j: (i, 0, group * N_PAIRS + j))

    return pl.pallas_call(
        functools.partial(_prompt_attn_kernel, seq=t),
        grid=(b, N_PAIRS),
        in_specs=[col_spec(4), col_spec(5), col_spec(6), _const_spec((N_HEADS, PAIR))],
        out_specs=pl.BlockSpec((1, t, PAIR), lambda i, j: (i, 0, j)),
        out_shape=jax.ShapeDtypeStruct((b, t, D_HEADS), BF16),
        scratch_shapes=[
            pltpu.VMEM((3, t, PAIR), F32),
            pltpu.VMEM((t, PAIR), BF16),
            pltpu.VMEM((t, PAIR), BF16),
            pltpu.VMEM((t, PAIR), BF16),
            pltpu.VMEM((3, t, PAIR), F32),
        ],
        compiler_params=_params(("arbitrary", "arbitrary")),
        name="prompt_attn",
    )(proj, proj, proj, slopes)


def _sample_tables(n_new, n_cache):
    slopes = _alibi_slopes()
    qpos = n_cache + jnp.arange(n_new)
    kpos = jnp.arange(n_cache + n_new)
    dist = qpos[:, None] - kpos[None, :]
    cnt = jnp.zeros(dist.shape, F32)
    for window, d in BRANCHES:
        cnt = cnt + ((dist >= 0) & (dist <= window) & (dist % d == 0)).astype(F32)
    bias = -slopes[:, None, None] * dist.astype(F32)[None]
    bias = jnp.where(cnt[None] > 0, bias, NEG_INF).reshape(N_HEADS * n_new, -1)
    cnt = jnp.broadcast_to(cnt[None], (N_HEADS,) + cnt.shape).reshape(N_HEADS * n_new, -1)
    return bias[:, :n_cache], cnt[:, :n_cache], bias[:, n_cache:], cnt[:, n_cache:]


def _sample_attn_kernel(q_ref, kn_ref, vn_ref, ck_ref, cv_ref, bc_ref, cc_ref, bn_ref, cn_ref,
                        o_ref, wk_ref, wv_ref, *, n_new, n_cache):
    keep = n_cache - n_new
    k_new = kn_ref[0]
    v_new = vn_ref[0]
    wk_ref[0, 0:keep, :] = ck_ref[0, n_new:n_cache, :]
    wk_ref[0, keep:n_cache, :] = k_new
    wv_ref[0, 0:keep, :] = cv_ref[0, n_new:n_cache, :]
    wv_ref[0, keep:n_cache, :] = v_new

    rows = N_HEADS * n_new
    q = q_ref[0]
    q_rows = jnp.concatenate([q] * N_HEADS, axis=0)
    row_head = lax.broadcasted_iota(jnp.int32, (rows, D_HEADS), 0) // n_new
    lane_head = lax.broadcasted_iota(jnp.int32, (rows, D_HEADS), 1) // HEAD_DIM
    mine = row_head == lane_head
    qm = jnp.where(mine, q_rows, 0.0).astype(BF16)

    s_c = _dot_nt(qm, ck_ref[0].astype(BF16)) + bc_ref[...]
    s_n = _dot_nt(qm, k_new.astype(BF16)) + bn_ref[...]
    m = jnp.maximum(jnp.max(s_c, axis=-1, keepdims=True), jnp.max(s_n, axis=-1, keepdims=True))
    p_c = cc_ref[...] * jnp.exp(s_c - m)
    p_n = cn_ref[...] * jnp.exp(s_n - m)
    den = jnp.sum(p_c, axis=-1, keepdims=True) + jnp.sum(p_n, axis=-1, keepdims=True)
    o = _dot(p_c.astype(BF16), cv_ref[0].astype(BF16)) + _dot(p_n.astype(BF16), v_new.astype(BF16))
    o = jnp.where(mine, o / den, 0.0)
    acc = o[0:n_new]
    for h in range(1, N_HEADS):
        acc = acc + o[h * n_new:(h + 1) * n_new]
    o_ref[0] = acc.astype(o_ref.dtype)


def _sample_attn(proj, cache_k, cache_v):
    b, n_new, _ = proj.shape
    n_cache = cache_k.shape[1]
    assert n_cache == MAX_WINDOW and n_new % 8 == 0
    rows = N_HEADS * n_new
    tables = _sample_tables(n_new, n_cache)

    def grp_spec(group):
        return pl.BlockSpec((1, n_new, D_HEADS), lambda i: (i, 0, group))

    cache_spec = pl.BlockSpec((1, n_cache, D_HEADS), lambda i: (i, 0, 0))
    return pl.pallas_call(
        functools.partial(_sample_attn_kernel, n_new=n_new, n_cache=n_cache),
        grid=(b,),
        in_specs=[grp_spec(4), grp_spec(5), grp_spec(6), cache_spec, cache_spec,
                  _const_spec((rows, n_cache)), _const_spec((rows, n_cache)),
                  _const_spec((rows, n_new)), _const_spec((rows, n_new))],
        out_specs=[pl.BlockSpec((1, n_new, D_HEADS), lambda i: (i, 0, 0)), cache_spec, cache_spec],
        out_shape=[
            jax.ShapeDtypeStruct((b, n_new, D_HEADS), BF16),
            jax.ShapeDtypeStruct((b, n_cache, D_HEADS), F32),
            jax.ShapeDtypeStruct((b, n_cache, D_HEADS), F32),
        ],
        compiler_params=_params(("arbitrary",)),
        name="sample_attn",
    )(proj, proj, proj, cache_k, cache_v, *tables)


FF_CHUNK = D_FF // 2
N_FF_CHUNKS = D_FF // FF_CHUNK
HALO = 8


def _mix_ffn_kernel(x_ref, ro_ref, ao_ref, wo_ref, n2_ref, wup_ref, cw_ref, cb_ref, wdn_ref, nf_ref, prev_ref,
                    y_ref, ua_ref, buf_ref, g_ref, *, tile, seq_in_tile):
    t = pl.program_id(1)
    x1 = x_ref[0] + _dot(ro_ref[0], wo_ref[0:D_HEADS, :]) + _dot(ao_ref[0], wo_ref[D_HEADS:2 * D_HEADS, :])
    h2 = _rmsnorm(x1, n2_ref[...]).astype(BF16)
    for c in range(N_FF_CHUNKS):
        cols = slice(c * FF_CHUNK, (c + 1) * FF_CHUNK)
        ua = _dot(h2, wup_ref[:, cols])
        ub = _dot(h2, wup_ref[:, D_FF + c * FF_CHUNK:D_FF + (c + 1) * FF_CHUNK])
        buf_ref[c, HALO:HALO + tile, :] = ua
        if seq_in_tile is None:
            @pl.when(t == 0)
            def _():
                buf_ref[c, 0:HALO, :] = prev_ref[0, :, cols]
            taps = [buf_ref[c, HALO - (CONV_W - 1 - j):HALO - (CONV_W - 1 - j) + tile, :] for j in range(CONV_W - 1)]
        else:
            buf_ref[c, 0:HALO, :] = jnp.zeros((HALO, FF_CHUNK), F32)
            pos = lax.broadcasted_iota(jnp.int32, (tile, FF_CHUNK), 0) % seq_in_tile
            taps = []
            for j in range(CONV_W - 1):
                back = CONV_W - 1 - j
                shifted = buf_ref[c, HALO - back:HALO - back + tile, :]
                taps.append(jnp.where(pos < back, prev_ref[j, :, cols], shifted))
        conv = cb_ref[:, cols]
        for j in range(CONV_W - 1):
            conv = conv + taps[j] * cw_ref[j:j + 1, cols]
        conv = conv + ua * cw_ref[CONV_W - 1:CONV_W, cols]
        if seq_in_tile is None:
            buf_ref[c, 0:HALO, :] = buf_ref[c, tile:tile + HALO, :]
            ua_ref[0, :, cols] = ua[tile - HALO:tile]
        else:
            ua_ref[0, :, cols] = ua
        g_ref[:, cols] = (_silu(conv) * ub).astype(BF16)
    y_ref[0] = _rmsnorm(x1 + _dot(g_ref[...], wdn_ref[...]), nf_ref[...])


def _mix_ffn(x, ro, ao, w_out, norm2_w, w_up, conv_w, conv_b, w_down, normf_w, prev, *, tile, seq_in_tile):
    b, t, _ = x.shape
    assert t % tile == 0
    if seq_in_tile is None:
        prev_spec = pl.BlockSpec((1, HALO, D_FF), lambda i, j: (i, 0, 0))
        ua_rows = HALO
    else:
        assert b == 1 and t == tile
        prev_spec = _const_spec((CONV_W - 1, tile, D_FF))
        ua_rows = tile
    return pl.pallas_call(
        functools.partial(_mix_ffn_kernel, tile=tile, seq_in_tile=seq_in_tile),
        grid=(b, t // tile),
        in_specs=[
            pl.BlockSpec((1, tile, D_MODEL), lambda i, j: (i, j, 0)),
            pl.BlockSpec((1, tile, D_HEADS), lambda i, j: (i, j, 0)),
            pl.BlockSpec((1, tile, D_HEADS), lambda i, j: (i, j, 0)),
            _const_spec((2 * D_HEADS, D_MODEL)),
            _const_spec((1, D_MODEL)),
            _const_spec((D_MODEL, 2 * D_FF)),
            _const_spec((CONV_W, D_FF)),
            _const_spec((1, D_FF)),
            _const_spec((D_FF, D_MODEL)),
            _const_spec((1, D_MODEL)),
            prev_spec,
        ],
        out_specs=[
            pl.BlockSpec((1, tile, D_MODEL), lambda i, j: (i, j, 0)),
            pl.BlockSpec((1, ua_rows, D_FF), lambda i, j: (i, 0, 0)),
        ],
        out_shape=[
            jax.ShapeDtypeStruct((b, t, D_MODEL), F32),
            jax.ShapeDtypeStruct((b, ua_rows, D_FF), F32),
        ],
        scratch_shapes=[pltpu.VMEM((N_FF_CHUNKS, tile + HALO, FF_CHUNK), F32), pltpu.VMEM((tile, D_FF), BF16)],
        compiler_params=_params(("arbitrary", "arbitrary")),
        name="mix_ffn",
    )(x, ro, ao, w_out, norm2_w.reshape(1, D_MODEL), w_up, conv_w, conv_b.reshape(1, D_FF), w_down,
      normf_w.reshape(1, D_MODEL), prev)


PROMPT_ROW_TILE = 256
RET_ROW_TILE = 512


def kernel(x_prompt, x_sample, state_ret, cache_win_k, cache_win_v, state_conv, norm1_w, w_in, ret_gn_w, ret_gn_b,
           w_out, norm2_w, w_up, conv_w, conv_b, w_down, normf_w):
    bp, tp, _ = x_prompt.shape
    bs, ts, _ = x_sample.shape
    n_cache = cache_win_k.shape[1]
    w_in_b, w_out_b, w_up_b, w_down_b = (w.astype(BF16) for w in (w_in, w_out, w_up, w_down))
    ffn_w = (w_out_b, norm2_w, w_up_b, conv_w, conv_b, w_down_b, normf_w)

    win = min(MAX_WINDOW, tp)
    proj_p, wk_p, wv_p = _inproj(x_prompt.reshape(bp * tp, D_MODEL), norm1_w, w_in_b, tile=PROMPT_ROW_TILE,
                                 out_dtype=BF16, seq_len=tp, win_rows=win)
    proj_p = proj_p.reshape(bp, tp, D_IN)
    ro_p, ret_p = _retention(proj_p, jnp.zeros((bp, N_HEADS, HEAD_DIM, HEAD_DIM), F32), ret_gn_w, ret_gn_b,
                             tile=RET_ROW_TILE)
    ao_p = _prompt_attn(proj_p)
    y_p, ua_tail_p = _mix_ffn(x_prompt, ro_p, ao_p, *ffn_w, jnp.zeros((bp, HALO, D_FF), F32),
                              tile=PROMPT_ROW_TILE, seq_in_tile=None)
    conv_p = ua_tail_p[:, HALO - (CONV_W - 1):]

    (proj_s,) = _inproj(x_sample.reshape(bs * ts, D_MODEL), norm1_w, w_in_b, tile=bs * ts, out_dtype=F32)
    proj_s = proj_s.reshape(bs, ts, D_IN)
    ro_s, ret_s = _retention(proj_s, state_ret, ret_gn_w, ret_gn_b, tile=ts)
    ao_s, wk_s, wv_s = _sample_attn(proj_s, cache_win_k.reshape(bs, n_cache, D_HEADS),
                                    cache_win_v.reshape(bs, n_cache, D_HEADS))
    pad = jnp.zeros((bs, ts - (CONV_W - 1), D_FF), F32)
    prev_s = jnp.stack([
        jnp.concatenate([state_conv[:, j:], pad, jnp.zeros((bs, j, D_FF), F32)], axis=1).reshape(bs * ts, D_FF)
        for j in range(CONV_W - 1)])
    y_s, ua_s = _mix_ffn(x_sample.reshape(1, bs * ts, D_MODEL), ro_s.reshape(1, bs * ts, D_HEADS),
                         ao_s.reshape(1, bs * ts, D_HEADS), *ffn_w, prev_s, tile=bs * ts, seq_in_tile=ts)
    conv_s = ua_s.reshape(bs, ts, D_FF)[:, ts - (CONV_W - 1):]

    return (y_p, y_s.reshape(bs, ts, D_MODEL),
            ret_p.astype(x_prompt.dtype), ret_s.astype(state_ret.dtype),
            wk_p.reshape(bp, win, N_HEADS, HEAD_DIM), wv_p.reshape(bp, win, N_HEADS, HEAD_DIM),
            wk_s.reshape(bs, n_cache, N_HEADS, HEAD_DIM), wv_s.reshape(bs, n_cache, N_HEADS, HEAD_DIM),
            conv_p, conv_s)
```

```python
import functools
import math

import jax
import jax.numpy as jnp
from jax import lax
from jax.experimental import pallas as pl
from jax.experimental.pallas import tpu as pltpu

F32 = jnp.float32
BF16 = jnp.bfloat16

D_MODEL = 1024
N_HEADS = 8
HEAD_DIM = 64
D_HEADS = N_HEADS * HEAD_DIM
N_GROUPS = 7
D_IN = N_GROUPS * D_HEADS
PAIR = 2 * HEAD_DIM
N_PAIRS = N_HEADS // 2
RET_CHUNK = 128
BRANCHES = ((128, 1), (512, 4), (2048, 16))
MAX_WINDOW = 2048
Q_TILE = 128
D_FF = 2816
CONV_W = 3
EPS = 1e-6
NEG_INF = float("-inf")

V7X_VMEM_LIMIT = 56 * 1024 * 1024


def _params(sem, vmem=V7X_VMEM_LIMIT):
    return pltpu.CompilerParams(dimension_semantics=sem, vmem_limit_bytes=vmem)


def _const_spec(shape):
    nd = len(shape)
    return pl.BlockSpec(shape, lambda *_: (0,) * nd, pipeline_mode=pl.Buffered(1))


def _rmsnorm(x, w):
    return x * lax.rsqrt(jnp.mean(x * x, axis=-1, keepdims=True) + EPS) * w


def _silu(x):
    return x * jax.nn.sigmoid(x)


def _dot(a, b):
    return jnp.dot(a, b, preferred_element_type=F32)


def _dot_nt(a, b):
    return lax.dot_general(a, b, (((1,), (1,)), ((), ())), preferred_element_type=F32)


def _dot_tn(a, b):
    return lax.dot_general(a, b, (((0,), (0,)), ((), ())), preferred_element_type=F32)


def _inproj_kernel(x_ref, nw_ref, w_ref, proj_ref, *win_refs, win_first_tile, tiles_per_seq):
    x = x_ref[...]
    h = _rmsnorm(x, nw_ref[...]).astype(BF16)
    for g in range(N_GROUPS):
        cols = slice(g * D_HEADS, (g + 1) * D_HEADS)
        r = _dot(h, w_ref[:, cols])
        if g in (1, 4):
            r = r * (HEAD_DIM ** -0.5)
        proj_ref[:, cols] = r.astype(proj_ref.dtype)
        if win_refs and g in (5, 6):
            out = win_refs[g - 5]

            @pl.when(pl.program_id(0) % tiles_per_seq >= win_first_tile)
            def _():
                out[0] = r.T


def _inproj(x2d, norm_w, w_bf16, *, tile, out_dtype, seq_len=None, win_rows=None):
    n = x2d.shape[0]
    assert n % tile == 0
    in_specs = [
        pl.BlockSpec((tile, D_MODEL), lambda i: (i, 0)),
        _const_spec((1, D_MODEL)),
        _const_spec((D_MODEL, D_IN)),
    ]
    out_shape = [jax.ShapeDtypeStruct((n, D_IN), out_dtype)]
    out_specs = [pl.BlockSpec((tile, D_IN), lambda i: (i, 0))]
    kw = dict(win_first_tile=0, tiles_per_seq=1)
    if win_rows is not None:
        assert seq_len % tile == 0 and win_rows % tile == 0
        tps = seq_len // tile
        first = (seq_len - win_rows) // tile
        kw = dict(win_first_tile=first, tiles_per_seq=tps)

        def win_map(i):
            return (i // tps, 0, jnp.maximum(i % tps - first, 0))

        for _ in range(2):
            out_shape.append(jax.ShapeDtypeStruct((n // seq_len, D_HEADS, win_rows), F32))
            out_specs.append(pl.BlockSpec((1, D_HEADS, tile), win_map))
    return pl.pallas_call(
        functools.partial(_inproj_kernel, **kw),
        grid=(n // tile,),
        in_specs=in_specs,
        out_specs=out_specs,
        out_shape=out_shape,
        compiler_params=_params(("arbitrary",)),
        name="inproj",
    )(x2d, norm_w.reshape(1, D_MODEL), w_bf16)


def _retention_tables(chunk):
    lg = jnp.log1p(-jnp.exp2(-5.0 - jnp.arange(N_HEADS, dtype=F32)))
    i = jnp.arange(chunk, dtype=F32)
    diff = i[:, None] - i[None, :]
    intra = jnp.where(diff >= 0, jnp.exp(lg[:, None, None] * jnp.maximum(diff, 0.0)), 0.0)
    lane_head = jnp.arange(D_HEADS) // HEAD_DIM
    q_dec = jnp.exp(lg[lane_head][None, :] * (i[:, None] + 1.0))
    k_dec = jnp.exp(lg[lane_head][None, :] * (chunk - 1.0 - i[:, None]))
    c_dec = jnp.exp(lg * chunk)
    row_head = jnp.arange(PAIR) // HEAD_DIM
    same = (row_head[:, None] == row_head[None, :]).astype(F32)
    c_pair = c_dec.reshape(N_PAIRS, 2)[:, row_head][:, :, None] * same[None]
    return intra, q_dec, k_dec, c_pair, same


def _retention_kernel(proj_ref, s0_ref, intra_ref, qdec_ref, kdec_ref, cpair_ref, same_ref, gw_ref, gb_ref,
                      ro_ref, sout_ref, state_ref, *, chunk, chunks_per_tile):
    t = pl.program_id(1)

    @pl.when(t == 0)
    def _():
        state_ref[...] = s0_ref[0]

    left = lax.broadcasted_iota(jnp.int32, (chunk, PAIR), 1) < HEAD_DIM
    for c in range(chunks_per_tile):
        rows = slice(c * chunk, (c + 1) * chunk)
        for p in range(N_PAIRS):
            lanes = slice(p * PAIR, (p + 1) * PAIR)

            def grp(g):
                return proj_ref[0, rows, g * D_HEADS + p * PAIR:g * D_HEADS + (p + 1) * PAIR]

            q2 = grp(0).astype(BF16)
            k2 = grp(1).astype(BF16)
            v2 = grp(2).astype(BF16)
            gate = grp(3).astype(F32)
            s_pair = state_ref[p]
            o = None
            for hh in range(2):
                mine = left if hh == 0 else jnp.logical_not(left)
                qm = jnp.where(mine, q2, jnp.zeros_like(q2))
                sc = _dot_nt(qm, k2) * intra_ref[2 * p + hh]
                oh = _dot(sc.astype(BF16), v2)
                o = oh if o is None else jnp.where(left, o, oh)
            o = o + _dot(q2, s_pair.astype(BF16)) * qdec_ref[:, lanes]
            kd = (k2.astype(F32) * kdec_ref[:, lanes]).astype(BF16)
            state_ref[p] = s_pair * cpair_ref[p] + _dot_tn(kd, v2) * same_ref[...]

            def head_mean(a):
                zero = jnp.zeros_like(a)
                sl = jnp.sum(jnp.where(left, a, zero), axis=-1, keepdims=True)
                sr = jnp.sum(jnp.where(left, zero, a), axis=-1, keepdims=True)
                return jnp.where(left, sl, sr) * (1.0 / HEAD_DIM)

            dlt = o - head_mean(o)
            var = head_mean(dlt * dlt)
            y = dlt * lax.rsqrt(var + EPS) * gw_ref[:, lanes] + gb_ref[:, lanes]
            ro_ref[0, rows, lanes] = (_silu(gate) * y).astype(ro_ref.dtype)

    @pl.when(t == pl.num_programs(1) - 1)
    def _():
        sout_ref[0] = state_ref[...]


def _pair_states(s):
    b = s.shape[0]
    s = s.astype(F32).reshape(b, N_PAIRS, 2, HEAD_DIM, HEAD_DIM)
    z = jnp.zeros_like(s[:, :, 0])
    top = jnp.concatenate([s[:, :, 0], z], axis=-1)
    bot = jnp.concatenate([z, s[:, :, 1]], axis=-1)
    return jnp.concatenate([top, bot], axis=-2)


def _unpair_states(sp):
    a = sp[:, :, :HEAD_DIM, :HEAD_DIM]
    d = sp[:, :, HEAD_DIM:, HEAD_DIM:]
    b = sp.shape[0]
    return jnp.stack([a, d], axis=2).reshape(b, N_HEADS, HEAD_DIM, HEAD_DIM)


def _retention(proj, s0, gn_w, gb_b, *, tile):
    b, t, _ = proj.shape
    chunk = math.gcd(t, RET_CHUNK)
    assert t % tile == 0 and tile % chunk == 0
    intra, q_dec, k_dec, c_pair, same = _retention_tables(chunk)
    ro, s_out = pl.pallas_call(
        functools.partial(_retention_kernel, chunk=chunk, chunks_per_tile=tile // chunk),
        grid=(b, t // tile),
        in_specs=[
            pl.BlockSpec((1, tile, 4 * D_HEADS), lambda i, j: (i, j, 0)),
            pl.BlockSpec((1, N_PAIRS, PAIR, PAIR), lambda i, j: (i, 0, 0, 0)),
            _const_spec((N_HEADS, chunk, chunk)),
            _const_spec((chunk, D_HEADS)),
            _const_spec((chunk, D_HEADS)),
            _const_spec((N_PAIRS, PAIR, PAIR)),
            _const_spec((PAIR, PAIR)),
            _const_spec((1, D_HEADS)),
            _const_spec((1, D_HEADS)),
        ],
        out_specs=[
            pl.BlockSpec((1, tile, D_HEADS), lambda i, j: (i, j, 0)),
            pl.BlockSpec((1, N_PAIRS, PAIR, PAIR), lambda i, j: (i, 0, 0, 0)),
        ],
        out_shape=[
            jax.ShapeDtypeStruct((b, t, D_HEADS), BF16),
            jax.ShapeDtypeStruct((b, N_PAIRS, PAIR, PAIR), F32),
        ],
        scratch_shapes=[pltpu.VMEM((N_PAIRS, PAIR, PAIR), F32)],
        compiler_params=_params(("arbitrary", "arbitrary")),
        name="retention",
    )(proj, _pair_states(s0), intra, q_dec, k_dec, c_pair, same,
      gn_w.reshape(1, D_HEADS).astype(F32), gb_b.reshape(1, D_HEADS).astype(F32))
    return ro, _unpair_states(s_out)


def _alibi_slopes():
    return jnp.exp2(-8.0 * jnp.arange(1, N_HEADS + 1, dtype=F32) / N_HEADS)


def _prompt_attn_kernel(q_ref, k_ref, v_ref, slope_ref, o_ref, xf_ref, qp_ref, kp_ref, vp_ref, st_ref, *, seq):
    hp = pl.program_id(1)
    left = lax.broadcasted_iota(jnp.int32, (Q_TILE, PAIR), 1) < HEAD_DIM
    ones = jnp.ones((2 * Q_TILE, PAIR), BF16)

    xf_ref[0] = q_ref[0].astype(F32)
    xf_ref[1] = k_ref[0].astype(F32)
    xf_ref[2] = v_ref[0].astype(F32)

    def tile(q, k, v, biases):
        nk = k.shape[0]
        vaug = jnp.concatenate([v, ones[:nk]], axis=1)
        res = []
        for hh in range(2):
            mine = left if hh == 0 else jnp.logical_not(left)
            qm = jnp.where(mine, q, jnp.zeros_like(q))
            s = _dot_nt(qm, k) + biases[hh]
            m = jnp.max(s, axis=-1, keepdims=True)
            p = jnp.exp(s - m).astype(BF16)
            pv = _dot(p, vaug)
            res.append((jnp.broadcast_to(m, (Q_TILE, PAIR)), pv[:, PAIR:], pv[:, :PAIR]))
        return tuple(jnp.where(left, a, b) for a, b in zip(res[0], res[1]))

    for bi, (window, d) in enumerate(BRANCHES):
        assert window // d == Q_TILE
        seg = seq // d
        tiles_per_seg = seg // Q_TILE
        qi = lax.broadcasted_iota(jnp.int32, (Q_TILE, 2 * Q_TILE), 0)
        kj = lax.broadcasted_iota(jnp.int32, (Q_TILE, 2 * Q_TILE), 1)
        n_rest = qi + Q_TILE - kj
        n_first = n_rest[:, :Q_TILE] - Q_TILE
        bias_rest, bias_first = [], []
        for hh in range(2):
            sl = slope_ref[pl.ds(2 * hp + hh, 1), :]
            sl2 = jnp.concatenate([sl, sl], axis=1)
            pen_rest = sl2 * (d * n_rest).astype(F32)
            pen_first = sl * (d * n_first).astype(F32)
            bias_rest.append(jnp.where((n_rest >= 0) & (n_rest <= Q_TILE), -pen_rest, NEG_INF))
            bias_first.append(jnp.where(n_first >= 0, -pen_first, NEG_INF))

        if d == 1:
            qs, ks, vs = q_ref.at[0], k_ref.at[0], v_ref.at[0]
        else:
            qs, ks, vs = qp_ref, kp_ref, vp_ref
            for r in range(d):
                dst = slice(r * seg, (r + 1) * seg)
                qp_ref[dst, :] = xf_ref[0, pl.ds(r, seg, stride=d), :].astype(BF16)
                kp_ref[dst, :] = xf_ref[1, pl.ds(r, seg, stride=d), :].astype(BF16)
                vp_ref[dst, :] = xf_ref[2, pl.ds(r, seg, stride=d), :].astype(BF16)

        def merge(r, tile_in_seg, stats, *, d=d, first_branch=(bi == 0)):
            m, den, num = stats
            start = d * Q_TILE * tile_in_seg + r
            idx = pl.ds(start, Q_TILE) if d == 1 else pl.ds(start, Q_TILE, stride=d)
            if first_branch:
                st_ref[0, idx, :] = m
                st_ref[1, idx, :] = den
                st_ref[2, idx, :] = num
            else:
                m0 = st_ref[0, idx, :]
                mm = jnp.maximum(m0, m)
                w0 = jnp.exp(m0 - mm)
                w1 = jnp.exp(m - mm)
                st_ref[0, idx, :] = mm
                st_ref[1, idx, :] = st_ref[1, idx, :] * w0 + den * w1
                st_ref[2, idx, :] = st_ref[2, idx, :] * w0 + num * w1

        def seg_body(r, carry, *, seg=seg, tiles_per_seg=tiles_per_seg, qs=qs, ks=ks, vs=vs,
                     bias_first=bias_first, bias_rest=bias_rest, merge=merge):
            base = pl.multiple_of(r * seg, Q_TILE)
            rows0 = pl.ds(base, Q_TILE)
            merge(r, 0, tile(qs[rows0, :], ks[rows0, :], vs[rows0, :], bias_first))

            def tile_body(j, carry2):
                q0 = pl.multiple_of(base + j * Q_TILE, Q_TILE)
                kv_rows = pl.ds(q0 - Q_TILE, 2 * Q_TILE)
                merge(r, j, tile(qs[pl.ds(q0, Q_TILE), :], ks[kv_rows, :], vs[kv_rows, :], bias_rest))
                return carry2

            return lax.fori_loop(1, tiles_per_seg, tile_body, carry)

        lax.fori_loop(0, d, seg_body, 0)

    o_ref[0] = (st_ref[2] / st_ref[1]).astype(o_ref.dtype)


def _prompt_attn(proj):
    b, t, _ = proj.shape
    assert t % (BRANCHES[-1][1] * Q_TILE) == 0
    slopes = jnp.broadcast_to(_alibi_slopes()[:, None], (N_HEADS, PAIR))

    def col_spec(group):
        return pl.BlockSpec((1, t, PAIR), lambda i, j: (i, 0, group * N_PAIRS + j))

    return pl.pallas_call(
        functools.partial(_prompt_attn_kernel, seq=t),
        grid=(b, N_PAIRS),
        in_specs=[col_spec(4), col_spec(5), col_spec(6), _const_spec((N_HEADS, PAIR))],
        out_specs=pl.BlockSpec((1, t, PAIR), lambda i, j: (i, 0, j)),
        out_shape=jax.ShapeDtypeStruct((b, t, D_HEADS), BF16),
        scratch_shapes=[
            pltpu.VMEM((3, t, PAIR), F32),
            pltpu.VMEM((t, PAIR), BF16),
            pltpu.VMEM((t, PAIR), BF16),
            pltpu.VMEM((t, PAIR), BF16),
            pltpu.VMEM((3, t, PAIR), F32),
        ],
        compiler_params=_params(("arbitrary", "arbitrary")),
        name="prompt_attn",
    )(proj, proj, proj, slopes)


def _sample_tables(n_new, n_cache):
    slopes = _alibi_slopes()
    qpos = n_cache + jnp.arange(n_new)
    kpos = jnp.arange(n_cache + n_new)
    dist = qpos[:, None] - kpos[None, :]
    cnt = jnp.zeros(dist.shape, F32)
    for window, d in BRANCHES:
        cnt = cnt + ((dist >= 0) & (dist <= window) & (dist % d == 0)).astype(F32)
    bias = -slopes[:, None, None] * dist.astype(F32)[None]
    bias = jnp.where(cnt[None] > 0, bias, NEG_INF).reshape(N_HEADS * n_new, -1)
    return bias[:, :n_cache], cnt[:, :n_cache], bias[:, n_cache:], cnt[:, n_cache:]


def _sample_attn_kernel(q_ref, kn_ref, vn_ref, knt_ref, vnt_ref, ck_ref, cv_ref, bc_ref, cc_ref, bn_ref, cn_ref,
                        o_ref, wk_ref, wv_ref, *, n_new, n_cache):
    keep = n_cache - n_new
    for h in range(N_HEADS):
        lanes = slice(h * HEAD_DIM, (h + 1) * HEAD_DIM)
        rows = slice(h * n_new, (h + 1) * n_new)
        kt = ck_ref[0, h]
        vt = cv_ref[0, h]
        wk_ref[0, h] = pltpu.roll(kt, keep, 1)
        wv_ref[0, h] = pltpu.roll(vt, keep, 1)
        wk_ref[0, h, :, keep:n_cache] = knt_ref[0, lanes, :]
        wv_ref[0, h, :, keep:n_cache] = vnt_ref[0, lanes, :]

        q = q_ref[0, :, lanes].astype(BF16)
        k_new = kn_ref[0, :, lanes].astype(BF16)
        v_new = vn_ref[0, :, lanes].astype(BF16)
        s_c = _dot(q, kt.astype(BF16)) + bc_ref[rows, :]
        s_n = _dot_nt(q, k_new) + bn_ref[rows, :]
        m = jnp.maximum(jnp.max(s_c, axis=-1, keepdims=True), jnp.max(s_n, axis=-1, keepdims=True))
        p_c = cc_ref[...] * jnp.exp(s_c - m)
        p_n = cn_ref[...] * jnp.exp(s_n - m)
        den = jnp.sum(p_c, axis=-1, keepdims=True) + jnp.sum(p_n, axis=-1, keepdims=True)
        o = _dot_nt(p_c.astype(BF16), vt.astype(BF16)) + _dot(p_n.astype(BF16), v_new)
        o_ref[0, :, lanes] = (o / den).astype(o_ref.dtype)


def _sample_attn(proj, knt, vnt, cache_kt, cache_vt):
    b, n_new, _ = proj.shape
    n_cache = cache_kt.shape[-1]
    assert n_cache == MAX_WINDOW and n_new % 8 == 0
    rows = N_HEADS * n_new
    tables = _sample_tables(n_new, n_cache)

    def grp_spec(group):
        return pl.BlockSpec((1, n_new, D_HEADS), lambda i: (i, 0, group))

    new_t_spec = pl.BlockSpec((1, D_HEADS, n_new), lambda i: (i, 0, 0))
    cache_spec = pl.BlockSpec((1, N_HEADS, HEAD_DIM, n_cache), lambda i: (i, 0, 0, 0))
    return pl.pallas_call(
        functools.partial(_sample_attn_kernel, n_new=n_new, n_cache=n_cache),
        grid=(b,),
        in_specs=[grp_spec(4), grp_spec(5), grp_spec(6), new_t_spec, new_t_spec, cache_spec, cache_spec,
                  _const_spec((rows, n_cache)), _const_spec((n_new, n_cache)),
                  _const_spec((rows, n_new)), _const_spec((n_new, n_new))],
        out_specs=[pl.BlockSpec((1, n_new, D_HEADS), lambda i: (i, 0, 0)), cache_spec, cache_spec],
        out_shape=[
            jax.ShapeDtypeStruct((b, n_new, D_HEADS), BF16),
            jax.ShapeDtypeStruct((b, N_HEADS, HEAD_DIM, n_cache), F32),
            jax.ShapeDtypeStruct((b, N_HEADS, HEAD_DIM, n_cache), F32),
        ],
        compiler_params=_params(("arbitrary",)),
        name="sample_attn",
    )(proj, proj, proj, knt, vnt, cache_kt, cache_vt, *tables)


FF_CHUNK = D_FF // 2
N_FF_CHUNKS = D_FF // FF_CHUNK
HALO = 8


def _mix_ffn_kernel(x_ref, ro_ref, ao_ref, wo_ref, n2_ref, wup_ref, cw_ref, cb_ref, wdn_ref, nf_ref, prev_ref,
                    y_ref, ua_ref, buf_ref, g_ref, *, tile, seq_in_tile):
    t = pl.program_id(1)
    x1 = x_ref[0] + _dot(ro_ref[0], wo_ref[0:D_HEADS, :]) + _dot(ao_ref[0], wo_ref[D_HEADS:2 * D_HEADS, :])
    h2 = _rmsnorm(x1, n2_ref[...]).astype(BF16)
    for c in range(N_FF_CHUNKS):
        cols = slice(c * FF_CHUNK, (c + 1) * FF_CHUNK)
        ua = _dot(h2, wup_ref[:, cols])
        ub = _dot(h2, wup_ref[:, D_FF + c * FF_CHUNK:D_FF + (c + 1) * FF_CHUNK])
        buf_ref[c, HALO:HALO + tile, :] = ua
        if seq_in_tile is None:
            @pl.when(t == 0)
            def _():
                buf_ref[c, 0:HALO, :] = prev_ref[0, :, cols]
            taps = [buf_ref[c, HALO - (CONV_W - 1 - j):HALO - (CONV_W - 1 - j) + tile, :] for j in range(CONV_W - 1)]
        else:
            buf_ref[c, 0:HALO, :] = jnp.zeros((HALO, FF_CHUNK), F32)
            pos = lax.broadcasted_iota(jnp.int32, (tile, FF_CHUNK), 0) % seq_in_tile
            taps = []
            for j in range(CONV_W - 1):
                back = CONV_W - 1 - j
                shifted = buf_ref[c, HALO - back:HALO - back + tile, :]
                taps.append(jnp.where(pos < back, prev_ref[j, :, cols], shifted))
        conv = cb_ref[:, cols]
        for j in range(CONV_W - 1):
            conv = conv + taps[j] * cw_ref[j:j + 1, cols]
        conv = conv + ua * cw_ref[CONV_W - 1:CONV_W, cols]
        if seq_in_tile is None:
            buf_ref[c, 0:HALO, :] = buf_ref[c, tile:tile + HALO, :]
            ua_ref[0, :, cols] = ua[tile - HALO:tile]
        else:
            ua_ref[0, :, cols] = ua
        g_ref[:, cols] = (_silu(conv) * ub).astype(BF16)
    y_ref[0] = _rmsnorm(x1 + _dot(g_ref[...], wdn_ref[...]), nf_ref[...])


def _mix_ffn(x, ro, ao, w_out, norm2_w, w_up, conv_w, conv_b, w_down, normf_w, prev, *, tile, seq_in_tile):
    b, t, _ = x.shape
    assert t % tile == 0
    if seq_in_tile is None:
        prev_spec = pl.BlockSpec((1, HALO, D_FF), lambda i, j: (i, 0, 0))
        ua_rows = HALO
    else:
        assert b == 1 and t == tile
        prev_spec = _const_spec((CONV_W - 1, tile, D_FF))
        ua_rows = tile
    return pl.pallas_call(
        functools.partial(_mix_ffn_kernel, tile=tile, seq_in_tile=seq_in_tile),
        grid=(b, t // tile),
        in_specs=[
            pl.BlockSpec((1, tile, D_MODEL), lambda i, j: (i, j, 0)),
            pl.BlockSpec((1, tile, D_HEADS), lambda i, j: (i, j, 0)),
            pl.BlockSpec((1, tile, D_HEADS), lambda i, j: (i, j, 0)),
            _const_spec((2 * D_HEADS, D_MODEL)),
            _const_spec((1, D_MODEL)),
            _const_spec((D_MODEL, 2 * D_FF)),
            _const_spec((CONV_W, D_FF)),
            _const_spec((1, D_FF)),
            _const_spec((D_FF, D_MODEL)),
            _const_spec((1, D_MODEL)),
            prev_spec,
        ],
        out_specs=[
            pl.BlockSpec((1, tile, D_MODEL), lambda i, j: (i, j, 0)),
            pl.BlockSpec((1, ua_rows, D_FF), lambda i, j: (i, 0, 0)),
        ],
        out_shape=[
            jax.ShapeDtypeStruct((b, t, D_MODEL), F32),
            jax.ShapeDtypeStruct((b, ua_rows, D_FF), F32),
        ],
        scratch_shapes=[pltpu.VMEM((N_FF_CHUNKS, tile + HALO, FF_CHUNK), F32), pltpu.VMEM((tile, D_FF), BF16)],
        compiler_params=_params(("arbitrary", "arbitrary")),
        name="mix_ffn",
    )(x, ro, ao, w_out, norm2_w.reshape(1, D_MODEL), w_up, conv_w, conv_b.reshape(1, D_FF), w_down,
      normf_w.reshape(1, D_MODEL), prev)


PROMPT_ROW_TILE = 256
RET_ROW_TILE = 512


def _rows_to_lanes(cache):
    return cache.transpose(0, 2, 3, 1)


def _lanes_to_rows(cache_t):
    return cache_t.transpose(0, 3, 1, 2)


def kernel(x_prompt, x_sample, state_ret, cache_win_k, cache_win_v, state_conv, norm1_w, w_in, ret_gn_w, ret_gn_b,
           w_out, norm2_w, w_up, conv_w, conv_b, w_down, normf_w):
    bp, tp, _ = x_prompt.shape
    bs, ts, _ = x_sample.shape
    n_cache = cache_win_k.shape[1]
    w_in_b, w_out_b, w_up_b, w_down_b = (w.astype(BF16) for w in (w_in, w_out, w_up, w_down))
    ffn_w = (w_out_b, norm2_w, w_up_b, conv_w, conv_b, w_down_b, normf_w)

    win = min(MAX_WINDOW, tp)
    proj_p, wk_p, wv_p = _inproj(x_prompt.reshape(bp * tp, D_MODEL), norm1_w, w_in_b, tile=PROMPT_ROW_TILE,
                                 out_dtype=BF16, seq_len=tp, win_rows=win)
    proj_p = proj_p.reshape(bp, tp, D_IN)
    ro_p, ret_p = _retention(proj_p, jnp.zeros((bp, N_HEADS, HEAD_DIM, HEAD_DIM), F32), ret_gn_w, ret_gn_b,
                             tile=RET_ROW_TILE)
    ao_p = _prompt_attn(proj_p)
    y_p, ua_tail_p = _mix_ffn(x_prompt, ro_p, ao_p, *ffn_w, jnp.zeros((bp, HALO, D_FF), F32),
                              tile=PROMPT_ROW_TILE, seq_in_tile=None)
    conv_p = ua_tail_p[:, HALO - (CONV_W - 1):]

    proj_s, knt, vnt = _inproj(x_sample.reshape(bs * ts, D_MODEL), norm1_w, w_in_b, tile=bs * ts, out_dtype=F32,
                               seq_len=bs * ts, win_rows=bs * ts)
    proj_s = proj_s.reshape(bs, ts, D_IN)
    knt, vnt = (a.reshape(D_HEADS, bs, ts).transpose(1, 0, 2) for a in (knt, vnt))
    ro_s, ret_s = _retention(proj_s, state_ret, ret_gn_w, ret_gn_b, tile=ts)
    ao_s, wk_s, wv_s = _sample_attn(proj_s, knt, vnt, _rows_to_lanes(cache_win_k), _rows_to_lanes(cache_win_v))
    pad = jnp.zeros((bs, ts - (CONV_W - 1), D_FF), F32)
    prev_s = jnp.stack([
        jnp.concatenate([state_conv[:, j:], pad, jnp.zeros((bs, j, D_FF), F32)], axis=1).reshape(bs * ts, D_FF)
        for j in range(CONV_W - 1)])
    y_s, ua_s = _mix_ffn(x_sample.reshape(1, bs * ts, D_MODEL), ro_s.reshape(1, bs * ts, D_HEADS),
                         ao_s.reshape(1, bs * ts, D_HEADS), *ffn_w, prev_s, tile=bs * ts, seq_in_tile=ts)
    conv_s = ua_s.reshape(bs, ts, D_FF)[:, ts - (CONV_W - 1):]

    return (y_p, y_s.reshape(bs, ts, D_MODEL),
            ret_p.astype(x_prompt.dtype), ret_s.astype(state_ret.dtype),
            _lanes_to_rows(wk_p.reshape(bp, N_HEADS, HEAD_DIM, win)),
            _lanes_to_rows(wv_p.reshape(bp, N_HEADS, HEAD_DIM, win)),
            _lanes_to_rows(wk_s), _lanes_to_rows(wv_s),
            conv_p, conv_s)
```

```python
import functools
import math

import jax
import jax.numpy as jnp
from jax import lax
from jax.experimental import pallas as pl
from jax.experimental.pallas import tpu as pltpu

F32 = jnp.float32
BF16 = jnp.bfloat16

D_MODEL = 1024
N_HEADS = 8
HEAD_DIM = 64
D_HEADS = N_HEADS * HEAD_DIM
N_GROUPS = 7
D_IN = N_GROUPS * D_HEADS
PAIR = 2 * HEAD_DIM
N_PAIRS = N_HEADS // 2
RET_CHUNK = 128
BRANCHES = ((128, 1), (512, 4), (2048, 16))
MAX_WINDOW = 2048
Q_TILE = 128
DIL_STEP = 4
ATTN_GROUP = 4
D_FF = 2816
CONV_W = 3
EPS = 1e-6
NEG_INF = float("-inf")

V7X_VMEM_LIMIT = 56 * 1024 * 1024


def _params(sem, vmem=V7X_VMEM_LIMIT):
    return pltpu.CompilerParams(dimension_semantics=sem, vmem_limit_bytes=vmem)


def _const_spec(shape):
    nd = len(shape)
    return pl.BlockSpec(shape, lambda *_: (0,) * nd, pipeline_mode=pl.Buffered(1))


def _rmsnorm(x, w):
    return x * lax.rsqrt(jnp.mean(x * x, axis=-1, keepdims=True) + EPS) * w


def _silu(x):
    return x * jax.nn.sigmoid(x)


def _dot(a, b):
    return jnp.dot(a, b, preferred_element_type=F32)


def _dot_nt(a, b):
    return lax.dot_general(a, b, (((1,), (1,)), ((), ())), preferred_element_type=F32)


def _dot_tn(a, b):
    return lax.dot_general(a, b, (((0,), (0,)), ((), ())), preferred_element_type=F32)


def _inproj_kernel(x_ref, nw_ref, w_ref, proj_ref, *win_refs, win_first_tile, tiles_per_seq):
    x = x_ref[...]
    h = _rmsnorm(x, nw_ref[...]).astype(BF16)
    for g in range(N_GROUPS):
        cols = slice(g * D_HEADS, (g + 1) * D_HEADS)
        r = _dot(h, w_ref[:, cols])
        if g in (1, 4):
            r = r * (HEAD_DIM ** -0.5)
        proj_ref[:, cols] = r.astype(proj_ref.dtype)
        if win_refs and g in (5, 6):
            out = win_refs[g - 5]

            @pl.when(pl.program_id(0) % tiles_per_seq >= win_first_tile)
            def _():
                out[0] = r.T


def _inproj(x2d, norm_w, w_bf16, *, tile, out_dtype, seq_len=None, win_rows=None):
    n = x2d.shape[0]
    assert n % tile == 0
    in_specs = [
        pl.BlockSpec((tile, D_MODEL), lambda i: (i, 0)),
        _const_spec((1, D_MODEL)),
        _const_spec((D_MODEL, D_IN)),
    ]
    out_shape = [jax.ShapeDtypeStruct((n, D_IN), out_dtype)]
    out_specs = [pl.BlockSpec((tile, D_IN), lambda i: (i, 0))]
    kw = dict(win_first_tile=0, tiles_per_seq=1)
    if win_rows is not None:
        assert seq_len % tile == 0 and win_rows % tile == 0
        tps = seq_len // tile
        first = (seq_len - win_rows) // tile
        kw = dict(win_first_tile=first, tiles_per_seq=tps)

        def win_map(i):
            return (i // tps, 0, jnp.maximum(i % tps - first, 0))

        for _ in range(2):
            out_shape.append(jax.ShapeDtypeStruct((n // seq_len, D_HEADS, win_rows), F32))
            out_specs.append(pl.BlockSpec((1, D_HEADS, tile), win_map))
    return pl.pallas_call(
        functools.partial(_inproj_kernel, **kw),
        grid=(n // tile,),
        in_specs=in_specs,
        out_specs=out_specs,
        out_shape=out_shape,
        compiler_params=_params(("arbitrary",)),
        name="inproj",
    )(x2d, norm_w.reshape(1, D_MODEL), w_bf16)


def _retention_tables(chunk):
    lg = jnp.log1p(-jnp.exp2(-5.0 - jnp.arange(N_HEADS, dtype=F32)))
    i = jnp.arange(chunk, dtype=F32)
    diff = i[:, None] - i[None, :]
    intra = jnp.where(diff >= 0, jnp.exp(lg[:, None, None] * jnp.maximum(diff, 0.0)), 0.0)
    lane_head = jnp.arange(D_HEADS) // HEAD_DIM
    q_dec = jnp.exp(lg[lane_head][None, :] * (i[:, None] + 1.0))
    k_dec = jnp.exp(lg[lane_head][None, :] * (chunk - 1.0 - i[:, None]))
    c_dec = jnp.exp(lg * chunk)
    row_head = jnp.arange(PAIR) // HEAD_DIM
    same = (row_head[:, None] == row_head[None, :]).astype(F32)
    c_pair = c_dec.reshape(N_PAIRS, 2)[:, row_head][:, :, None] * same[None]
    return intra, q_dec, k_dec, c_pair, same


def _retention_kernel(proj_ref, s0_ref, intra_ref, qdec_ref, kdec_ref, cpair_ref, same_ref, gw_ref, gb_ref,
                      ro_ref, sout_ref, state_ref, *, chunk, chunks_per_tile):
    t = pl.program_id(1)

    @pl.when(t == 0)
    def _():
        state_ref[...] = s0_ref[0]

    left = lax.broadcasted_iota(jnp.int32, (chunk, PAIR), 1) < HEAD_DIM
    for c in range(chunks_per_tile):
        rows = slice(c * chunk, (c + 1) * chunk)
        for p in range(N_PAIRS):
            lanes = slice(p * PAIR, (p + 1) * PAIR)

            def grp(g):
                return proj_ref[0, rows, g * D_HEADS + p * PAIR:g * D_HEADS + (p + 1) * PAIR]

            q2 = grp(0).astype(BF16)
            k2 = grp(1).astype(BF16)
            v2 = grp(2).astype(BF16)
            gate = grp(3).astype(F32)
            s_pair = state_ref[p]
            o = None
            for hh in range(2):
                mine = left if hh == 0 else jnp.logical_not(left)
                qm = jnp.where(mine, q2, jnp.zeros_like(q2))
                sc = _dot_nt(qm, k2) * intra_ref[2 * p + hh]
                oh = _dot(sc.astype(BF16), v2)
                o = oh if o is None else jnp.where(left, o, oh)
            o = o + _dot(q2, s_pair.astype(BF16)) * qdec_ref[:, lanes]
            kd = (k2.astype(F32) * kdec_ref[:, lanes]).astype(BF16)
            state_ref[p] = s_pair * cpair_ref[p] + _dot_tn(kd, v2) * same_ref[...]

            def head_mean(a):
                zero = jnp.zeros_like(a)
                sl = jnp.sum(jnp.where(left, a, zero), axis=-1, keepdims=True)
                sr = jnp.sum(jnp.where(left, zero, a), axis=-1, keepdims=True)
                return jnp.where(left, sl, sr) * (1.0 / HEAD_DIM)

            dlt = o - head_mean(o)
            var = head_mean(dlt * dlt)
            y = dlt * lax.rsqrt(var + EPS) * gw_ref[:, lanes] + gb_ref[:, lanes]
            ro_ref[0, rows, lanes] = (_silu(gate) * y).astype(ro_ref.dtype)

    @pl.when(t == pl.num_programs(1) - 1)
    def _():
        sout_ref[0] = state_ref[...]


def _pair_states(s):
    b = s.shape[0]
    s = s.astype(F32).reshape(b, N_PAIRS, 2, HEAD_DIM, HEAD_DIM)
    z = jnp.zeros_like(s[:, :, 0])
    top = jnp.concatenate([s[:, :, 0], z], axis=-1)
    bot = jnp.concatenate([z, s[:, :, 1]], axis=-1)
    return jnp.concatenate([top, bot], axis=-2)


def _unpair_states(sp):
    a = sp[:, :, :HEAD_DIM, :HEAD_DIM]
    d = sp[:, :, HEAD_DIM:, HEAD_DIM:]
    b = sp.shape[0]
    return jnp.stack([a, d], axis=2).reshape(b, N_HEADS, HEAD_DIM, HEAD_DIM)


def _retention(proj, s0, gn_w, gb_b, *, tile):
    b, t, _ = proj.shape
    chunk = math.gcd(t, RET_CHUNK)
    assert t % tile == 0 and tile % chunk == 0
    intra, q_dec, k_dec, c_pair, same = _retention_tables(chunk)
    ro, s_out = pl.pallas_call(
        functools.partial(_retention_kernel, chunk=chunk, chunks_per_tile=tile // chunk),
        grid=(b, t // tile),
        in_specs=[
            pl.BlockSpec((1, tile, 4 * D_HEADS), lambda i, j: (i, j, 0)),
            pl.BlockSpec((1, N_PAIRS, PAIR, PAIR), lambda i, j: (i, 0, 0, 0)),
            _const_spec((N_HEADS, chunk, chunk)),
            _const_spec((chunk, D_HEADS)),
            _const_spec((chunk, D_HEADS)),
            _const_spec((N_PAIRS, PAIR, PAIR)),
            _const_spec((PAIR, PAIR)),
            _const_spec((1, D_HEADS)),
            _const_spec((1, D_HEADS)),
        ],
        out_specs=[
            pl.BlockSpec((1, tile, D_HEADS), lambda i, j: (i, j, 0)),
            pl.BlockSpec((1, N_PAIRS, PAIR, PAIR), lambda i, j: (i, 0, 0, 0)),
        ],
        out_shape=[
            jax.ShapeDtypeStruct((b, t, D_HEADS), BF16),
            jax.ShapeDtypeStruct((b, N_PAIRS, PAIR, PAIR), F32),
        ],
        scratch_shapes=[pltpu.VMEM((N_PAIRS, PAIR, PAIR), F32)],
        compiler_params=_params(("arbitrary", "arbitrary")),
        name="retention",
    )(proj, _pair_states(s0), intra, q_dec, k_dec, c_pair, same,
      gn_w.reshape(1, D_HEADS).astype(F32), gb_b.reshape(1, D_HEADS).astype(F32))
    return ro, _unpair_states(s_out)


def _alibi_slopes():
    return jnp.exp2(-8.0 * jnp.arange(1, N_HEADS + 1, dtype=F32) / N_HEADS)


def _prompt_attn_kernel(q_ref, k_ref, v_ref, slope_ref, o_ref, xf_ref, xg_ref, qp_ref, kp_ref, vp_ref, st_ref, *, seq):
    hp = pl.program_id(1)
    left = lax.broadcasted_iota(jnp.int32, (Q_TILE, PAIR), 1) < HEAD_DIM
    ones = jnp.ones((2 * Q_TILE, PAIR), BF16)

    xf_ref[0] = q_ref[0].astype(F32)
    xf_ref[1] = k_ref[0].astype(F32)
    xf_ref[2] = v_ref[0].astype(F32)
    f32_bufs = (xf_ref, xg_ref)

    def run_group(tiles, biases, state_idx, first_branch, qs, ks, vs):
        ops = [(qs[pl.ds(q0, Q_TILE), :], ks[pl.ds(k0, nk), :], vs[pl.ds(k0, nk), :]) for q0, k0, nk in tiles]
        old = None if first_branch else [tuple(st_ref[c, idx, :] for c in range(3)) for idx in state_idx]
        scores = []
        for (q, k, _), bias in zip(ops, biases):
            for hh in range(2):
                mine = left if hh == 0 else jnp.logical_not(left)
                scores.append(_dot_nt(jnp.where(mine, q, jnp.zeros_like(q)), k) + bias[hh])
        probs = []
        for s in scores:
            m = jnp.max(s, axis=-1, keepdims=True)
            probs.append((m, jnp.exp(s - m).astype(BF16)))
        new = []
        for ti, (_, _, v) in enumerate(ops):
            vaug = jnp.concatenate([v, ones[:v.shape[0]]], axis=1)
            res = []
            for hh in range(2):
                m, p = probs[2 * ti + hh]
                pv = _dot(p, vaug)
                res.append((jnp.broadcast_to(m, (Q_TILE, PAIR)), pv[:, PAIR:], pv[:, :PAIR]))
            m, den, num = (jnp.where(left, a, b) for a, b in zip(res[0], res[1]))
            if first_branch:
                new.append((m, den, num))
            else:
                m0, den0, num0 = old[ti]
                mm = jnp.maximum(m0, m)
                w0 = jnp.exp(m0 - mm)
                w1 = jnp.exp(m - mm)
                new.append((mm, den0 * w0 + den * w1, num0 * w0 + num * w1))
        for idx, vals in zip(state_idx, new):
            for c in range(3):
                st_ref[c, idx, :] = vals[c]

    for bi, (window, d) in enumerate(BRANCHES):
        assert window // d == Q_TILE
        seg = seq // d
        tiles_per_seg = seg // Q_TILE
        group = min(ATTN_GROUP, tiles_per_seg)
        assert tiles_per_seg % group == 0
        qi = lax.broadcasted_iota(jnp.int32, (Q_TILE, 2 * Q_TILE), 0)
        kj = lax.broadcasted_iota(jnp.int32, (Q_TILE, 2 * Q_TILE), 1)
        n_rest = qi + Q_TILE - kj
        n_first = n_rest[:, :Q_TILE] - Q_TILE
        bias_rest, bias_first = [], []
        for hh in range(2):
            sl = slope_ref[pl.ds(2 * hp + hh, 1), :]
            sl2 = jnp.concatenate([sl, sl], axis=1)
            pen_rest = sl2 * (d * n_rest).astype(F32)
            pen_first = sl * (d * n_first).astype(F32)
            bias_rest.append(jnp.where((n_rest >= 0) & (n_rest <= Q_TILE), -pen_rest, NEG_INF))
            bias_first.append(jnp.where(n_first >= 0, -pen_first, NEG_INF))

        if d == 1:
            qs, ks, vs = q_ref.at[0], k_ref.at[0], v_ref.at[0]
        else:
            assert d == BRANCHES[bi - 1][1] * DIL_STEP
            qs, ks, vs = qp_ref, kp_ref, vp_ref
            src, dst = f32_bufs[(bi - 1) % 2], f32_bufs[bi % 2]
            last = bi == len(BRANCHES) - 1
            for s_prev in range(d // DIL_STEP):
                for u in range(DIL_STEP):
                    rows = slice((s_prev * DIL_STEP + u) * seg, (s_prev * DIL_STEP + u + 1) * seg)
                    for a, out in enumerate((qp_ref, kp_ref, vp_ref)):
                        val = src[a, pl.ds(s_prev * seg * DIL_STEP + u, seg, stride=DIL_STEP), :]
                        out[rows, :] = val.astype(BF16)
                        if not last:
                            dst[a, rows, :] = val

        def seg_body(r0, carry, *, d=d, seg=seg, tiles_per_seg=tiles_per_seg, group=group, qs=qs, ks=ks, vs=vs,
                     bias_first=bias_first, bias_rest=bias_rest, first_branch=(bi == 0)):
            segs_per_iter = max(1, ATTN_GROUP // tiles_per_seg)

            def residue(s, dil):
                if dil == 1:
                    return 0
                return (dil // DIL_STEP) * (s % DIL_STEP) + residue(s // DIL_STEP, dil // DIL_STEP)

            def tile_spec(s, tile_in_seg, is_first):
                q0 = pl.multiple_of(s * seg + tile_in_seg * Q_TILE, Q_TILE)
                start = d * Q_TILE * tile_in_seg + residue(s, d)
                rows = pl.ds(start, Q_TILE) if d == 1 else pl.ds(start, Q_TILE, stride=d)
                if is_first:
                    return (q0, q0, Q_TILE), bias_first, rows
                return (q0, q0 - Q_TILE, 2 * Q_TILE), bias_rest, rows

            def run(specs):
                tiles, biases, rows = zip(*specs)
                run_group(tiles, biases, rows, first_branch, qs, ks, vs)

            segs = [r0 * segs_per_iter + i for i in range(segs_per_iter)]
            run([tile_spec(r, u, u == 0) for r in segs for u in range(group)])

            def group_body(g, carry2):
                run([tile_spec(segs[0], g * group + u, False) for u in range(group)])
                return carry2

            if tiles_per_seg > group:
                carry = lax.fori_loop(1, tiles_per_seg // group, group_body, carry)
            return carry

        lax.fori_loop(0, d // max(1, ATTN_GROUP // tiles_per_seg), seg_body, 0)

    o_ref[0] = (st_ref[2] / st_ref[1]).astype(o_ref.dtype)


def _prompt_attn(proj):
    b, t, _ = proj.shape
    assert t % (BRANCHES[-1][1] * Q_TILE) == 0
    slopes = jnp.broadcast_to(_alibi_slopes()[:, None], (N_HEADS, PAIR))

    def col_spec(group):
        return pl.BlockSpec((1, t, PAIR), lambda i, j: (i, 0, group * N_PAIRS + j))

    return pl.pallas_call(
        functools.partial(_prompt_attn_kernel, seq=t),
        grid=(b, N_PAIRS),
        in_specs=[col_spec(4), col_spec(5), col_spec(6), _const_spec((N_HEADS, PAIR))],
        out_specs=pl.BlockSpec((1, t, PAIR), lambda i, j: (i, 0, j)),
        out_shape=jax.ShapeDtypeStruct((b, t, D_HEADS), BF16),
        scratch_shapes=[
            pltpu.VMEM((3, t, PAIR), F32),
            pltpu.VMEM((3, t, PAIR), F32),
            pltpu.VMEM((t, PAIR), BF16),
            pltpu.VMEM((t, PAIR), BF16),
            pltpu.VMEM((t, PAIR), BF16),
            pltpu.VMEM((3, t, PAIR), F32),
        ],
        compiler_params=_params(("arbitrary", "arbitrary")),
        name="prompt_attn",
    )(proj, proj, proj, slopes)


def _sample_tables(n_new, n_cache):
    slopes = _alibi_slopes()
    qpos = n_cache + jnp.arange(n_new)
    kpos = jnp.arange(n_cache + n_new)
    dist = qpos[:, None] - kpos[None, :]
    cnt = jnp.zeros(dist.shape, F32)
    for window, d in BRANCHES:
        cnt = cnt + ((dist >= 0) & (dist <= window) & (dist % d == 0)).astype(F32)
    bias = -slopes[:, None, None] * dist.astype(F32)[None]
    bias = jnp.where(cnt[None] > 0, bias, NEG_INF).reshape(N_HEADS * n_new, -1)
    return bias[:, :n_cache], cnt[:, :n_cache], bias[:, n_cache:], cnt[:, n_cache:]


def _sample_attn_kernel(q_ref, kn_ref, vn_ref, knt_ref, vnt_ref, ck_ref, cv_ref, bc_ref, cc_ref, bn_ref, cn_ref,
                        o_ref, wk_ref, wv_ref, *, n_new, n_cache):
    keep = n_cache - n_new
    for h in range(N_HEADS):
        lanes = slice(h * HEAD_DIM, (h + 1) * HEAD_DIM)
        rows = slice(h * n_new, (h + 1) * n_new)
        kt = ck_ref[0, h]
        vt = cv_ref[0, h]
        wk_ref[0, h] = pltpu.roll(kt, keep, 1)
        wv_ref[0, h] = pltpu.roll(vt, keep, 1)
        wk_ref[0, h, :, keep:n_cache] = knt_ref[0, lanes, :]
        wv_ref[0, h, :, keep:n_cache] = vnt_ref[0, lanes, :]

        q = q_ref[0, :, lanes].astype(BF16)
        k_new = kn_ref[0, :, lanes].astype(BF16)
        v_new = vn_ref[0, :, lanes].astype(BF16)
        s_c = _dot(q, kt.astype(BF16)) + bc_ref[rows, :]
        s_n = _dot_nt(q, k_new) + bn_ref[rows, :]
        m = jnp.maximum(jnp.max(s_c, axis=-1, keepdims=True), jnp.max(s_n, axis=-1, keepdims=True))
        p_c = cc_ref[...] * jnp.exp(s_c - m)
        p_n = cn_ref[...] * jnp.exp(s_n - m)
        den = jnp.sum(p_c, axis=-1, keepdims=True) + jnp.sum(p_n, axis=-1, keepdims=True)
        o = _dot_nt(p_c.astype(BF16), vt.astype(BF16)) + _dot(p_n.astype(BF16), v_new)
        o_ref[0, :, lanes] = (o / den).astype(o_ref.dtype)


def _sample_attn(proj, knt, vnt, cache_kt, cache_vt):
    b, n_new, _ = proj.shape
    n_cache = cache_kt.shape[-1]
    assert n_cache == MAX_WINDOW and n_new % 8 == 0
    rows = N_HEADS * n_new
    tables = _sample_tables(n_new, n_cache)

    def grp_spec(group):
        return pl.BlockSpec((1, n_new, D_HEADS), lambda i: (i, 0, group))

    new_t_spec = pl.BlockSpec((1, D_HEADS, n_new), lambda i: (i, 0, 0))
    cache_spec = pl.BlockSpec((1, N_HEADS, HEAD_DIM, n_cache), lambda i: (i, 0, 0, 0))
    return pl.pallas_call(
        functools.partial(_sample_attn_kernel, n_new=n_new, n_cache=n_cache),
        grid=(b,),
        in_specs=[grp_spec(4), grp_spec(5), grp_spec(6), new_t_spec, new_t_spec, cache_spec, cache_spec,
                  _const_spec((rows, n_cache)), _const_spec((n_new, n_cache)),
                  _const_spec((rows, n_new)), _const_spec((n_new, n_new))],
        out_specs=[pl.BlockSpec((1, n_new, D_HEADS), lambda i: (i, 0, 0)), cache_spec, cache_spec],
        out_shape=[
            jax.ShapeDtypeStruct((b, n_new, D_HEADS), BF16),
            jax.ShapeDtypeStruct((b, N_HEADS, HEAD_DIM, n_cache), F32),
            jax.ShapeDtypeStruct((b, N_HEADS, HEAD_DIM, n_cache), F32),
        ],
        compiler_params=_params(("arbitrary",)),
        name="sample_attn",
    )(proj, proj, proj, knt, vnt, cache_kt, cache_vt, *tables)


FF_CHUNK = D_FF // 2
N_FF_CHUNKS = D_FF // FF_CHUNK
HALO = 8


def _mix_ffn_kernel(x_ref, ro_ref, ao_ref, wo_ref, n2_ref, wup_ref, cw_ref, cb_ref, wdn_ref, nf_ref, prev_ref,
                    y_ref, ua_ref, buf_ref, g_ref, *, tile, seq_in_tile):
    t = pl.program_id(1)
    x1 = x_ref[0] + _dot(ro_ref[0], wo_ref[0:D_HEADS, :]) + _dot(ao_ref[0], wo_ref[D_HEADS:2 * D_HEADS, :])
    h2 = _rmsnorm(x1, n2_ref[...]).astype(BF16)
    for c in range(N_FF_CHUNKS):
        cols = slice(c * FF_CHUNK, (c + 1) * FF_CHUNK)
        ua = _dot(h2, wup_ref[:, cols])
        ub = _dot(h2, wup_ref[:, D_FF + c * FF_CHUNK:D_FF + (c + 1) * FF_CHUNK])
        buf_ref[c, HALO:HALO + tile, :] = ua
        if seq_in_tile is None:
            @pl.when(t == 0)
            def _():
                buf_ref[c, 0:HALO, :] = prev_ref[0, :, cols]
            taps = [buf_ref[c, HALO - (CONV_W - 1 - j):HALO - (CONV_W - 1 - j) + tile, :] for j in range(CONV_W - 1)]
        else:
            buf_ref[c, 0:HALO, :] = jnp.zeros((HALO, FF_CHUNK), F32)
            pos = lax.broadcasted_iota(jnp.int32, (tile, FF_CHUNK), 0) % seq_in_tile
            taps = []
            for j in range(CONV_W - 1):
                back = CONV_W - 1 - j
                shifted = buf_ref[c, HALO - back:HALO - back + tile, :]
                taps.append(jnp.where(pos < back, prev_ref[j, :, cols], shifted))
        conv = cb_ref[:, cols]
        for j in range(CONV_W - 1):
            conv = conv + taps[j] * cw_ref[j:j + 1, cols]
        conv = conv + ua * cw_ref[CONV_W - 1:CONV_W, cols]
        if seq_in_tile is None:
            buf_ref[c, 0:HALO, :] = buf_ref[c, tile:tile + HALO, :]
            ua_ref[0, :, cols] = ua[tile - HALO:tile]
        else:
            ua_ref[0, :, cols] = ua
        g_ref[:, cols] = (_silu(conv) * ub).astype(BF16)
    y_ref[0] = _rmsnorm(x1 + _dot(g_ref[...], wdn_ref[...]), nf_ref[...])


def _mix_ffn(x, ro, ao, w_out, norm2_w, w_up, conv_w, conv_b, w_down, normf_w, prev, *, tile, seq_in_tile):
    b, t, _ = x.shape
    assert t % tile == 0
    if seq_in_tile is None:
        prev_spec = pl.BlockSpec((1, HALO, D_FF), lambda i, j: (i, 0, 0))
        ua_rows = HALO
    else:
        assert b == 1 and t == tile
        prev_spec = _const_spec((CONV_W - 1, tile, D_FF))
        ua_rows = tile
    return pl.pallas_call(
        functools.partial(_mix_ffn_kernel, tile=tile, seq_in_tile=seq_in_tile),
        grid=(b, t // tile),
        in_specs=[
            pl.BlockSpec((1, tile, D_MODEL), lambda i, j: (i, j, 0)),
            pl.BlockSpec((1, tile, D_HEADS), lambda i, j: (i, j, 0)),
            pl.BlockSpec((1, tile, D_HEADS), lambda i, j: (i, j, 0)),
            _const_spec((2 * D_HEADS, D_MODEL)),
            _const_spec((1, D_MODEL)),
            _const_spec((D_MODEL, 2 * D_FF)),
            _const_spec((CONV_W, D_FF)),
            _const_spec((1, D_FF)),
            _const_spec((D_FF, D_MODEL)),
            _const_spec((1, D_MODEL)),
            prev_spec,
        ],
        out_specs=[
            pl.BlockSpec((1, tile, D_MODEL), lambda i, j: (i, j, 0)),
            pl.BlockSpec((1, ua_rows, D_FF), lambda i, j: (i, 0, 0)),
        ],
        out_shape=[
            jax.ShapeDtypeStruct((b, t, D_MODEL), F32),
            jax.ShapeDtypeStruct((b, ua_rows, D_FF), F32),
        ],
        scratch_shapes=[pltpu.VMEM((N_FF_CHUNKS, tile + HALO, FF_CHUNK), F32), pltpu.VMEM((tile, D_FF), BF16)],
        compiler_params=_params(("arbitrary", "arbitrary")),
        name="mix_ffn",
    )(x, ro, ao, w_out, norm2_w.reshape(1, D_MODEL), w_up, conv_w, conv_b.reshape(1, D_FF), w_down,
      normf_w.reshape(1, D_MODEL), prev)


PROMPT_ROW_TILE = 256
RET_ROW_TILE = 512


def _rows_to_lanes(cache):
    return cache.transpose(0, 2, 3, 1)


def _lanes_to_rows(cache_t):
    return cache_t.transpose(0, 3, 1, 2)


def kernel(x_prompt, x_sample, state_ret, cache_win_k, cache_win_v, state_conv, norm1_w, w_in, ret_gn_w, ret_gn_b,
           w_out, norm2_w, w_up, conv_w, conv_b, w_down, normf_w):
    bp, tp, _ = x_prompt.shape
    bs, ts, _ = x_sample.shape
    n_cache = cache_win_k.shape[1]
    w_in_b, w_out_b, w_up_b, w_down_b = (w.astype(BF16) for w in (w_in, w_out, w_up, w_down))
    ffn_w = (w_out_b, norm2_w, w_up_b, conv_w, conv_b, w_down_b, normf_w)

    win = min(MAX_WINDOW, tp)
    proj_p, wk_p, wv_p = _inproj(x_prompt.reshape(bp * tp, D_MODEL), norm1_w, w_in_b, tile=PROMPT_ROW_TILE,
                                 out_dtype=BF16, seq_len=tp, win_rows=win)
    proj_p = proj_p.reshape(bp, tp, D_IN)
    ro_p, ret_p = _retention(proj_p, jnp.zeros((bp, N_HEADS, HEAD_DIM, HEAD_DIM), F32), ret_gn_w, ret_gn_b,
                             tile=RET_ROW_TILE)
    ao_p = _prompt_attn(proj_p)
    y_p, ua_tail_p = _mix_ffn(x_prompt, ro_p, ao_p, *ffn_w, jnp.zeros((bp, HALO, D_FF), F32),
                              tile=PROMPT_ROW_TILE, seq_in_tile=None)
    conv_p = ua_tail_p[:, HALO - (CONV_W - 1):]

    proj_s, knt, vnt = _inproj(x_sample.reshape(bs * ts, D_MODEL), norm1_w, w_in_b, tile=bs * ts, out_dtype=F32,
                               seq_len=bs * ts, win_rows=bs * ts)
    proj_s = proj_s.reshape(bs, ts, D_IN)
    knt, vnt = (a.reshape(D_HEADS, bs, ts).transpose(1, 0, 2) for a in (knt, vnt))
    ro_s, ret_s = _retention(proj_s, state_ret, ret_gn_w, ret_gn_b, tile=ts)
    ao_s, wk_s, wv_s = _sample_attn(proj_s, knt, vnt, _rows_to_lanes(cache_win_k), _rows_to_lanes(cache_win_v))
    pad = jnp.zeros((bs, ts - (CONV_W - 1), D_FF), F32)
    prev_s = jnp.stack([
        jnp.concatenate([state_conv[:, j:], pad, jnp.zeros((bs, j, D_FF), F32)], axis=1).reshape(bs * ts, D_FF)
        for j in range(CONV_W - 1)])
    y_s, ua_s = _mix_ffn(x_sample.reshape(1, bs * ts, D_MODEL), ro_s.reshape(1, bs * ts, D_HEADS),
                         ao_s.reshape(1, bs * ts, D_HEADS), *ffn_w, prev_s, tile=bs * ts, seq_in_tile=ts)
    conv_s = ua_s.reshape(bs, ts, D_FF)[:, ts - (CONV_W - 1):]

    return (y_p, y_s.reshape(bs, ts, D_MODEL),
            ret_p.astype(x_prompt.dtype), ret_s.astype(state_ret.dtype),
            _lanes_to_rows(wk_p.reshape(bp, N_HEADS, HEAD_DIM, win)),
            _lanes_to_rows(wv_p.reshape(bp, N_HEADS, HEAD_DIM, win)),
            _lanes_to_rows(wk_s), _lanes_to_rows(wv_s),
            conv_p, conv_s)
```

```python
import functools
import math

import jax
import jax.numpy as jnp
import numpy as np
from jax import lax
from jax.experimental import pallas as pl
from jax.experimental.pallas import tpu as pltpu

F32 = jnp.float32
BF16 = jnp.bfloat16

D_MODEL = 1024
N_HEADS = 8
HEAD_DIM = 64
D_HEADS = N_HEADS * HEAD_DIM
N_GROUPS = 7
D_IN = N_GROUPS * D_HEADS
PAIR = 2 * HEAD_DIM
N_PAIRS = N_HEADS // 2
RET_CHUNK = 128
BRANCHES = ((128, 1), (512, 4), (2048, 16))
MAX_WINDOW = 2048
Q_TILE = 128
DIL_STEP = 4
ATTN_GROUPS = (8, 4, 8)
D_FF = 2816
CONV_W = 3
EPS = 1e-6
NEG_INF = float("-inf")

V7X_VMEM_LIMIT = 56 * 1024 * 1024


def _params(sem, vmem=V7X_VMEM_LIMIT):
    return pltpu.CompilerParams(dimension_semantics=sem, vmem_limit_bytes=vmem)


def _const_spec(shape):
    nd = len(shape)
    return pl.BlockSpec(shape, lambda *_: (0,) * nd, pipeline_mode=pl.Buffered(1))


def _rmsnorm(x, w):
    return x * lax.rsqrt(jnp.mean(x * x, axis=-1, keepdims=True) + EPS) * w


def _silu(x):
    return x * jax.nn.sigmoid(x)


def _dot(a, b):
    return jnp.dot(a, b, preferred_element_type=F32)


def _dot_nt(a, b):
    return lax.dot_general(a, b, (((1,), (1,)), ((), ())), preferred_element_type=F32)


def _dot_tn(a, b):
    return lax.dot_general(a, b, (((0,), (0,)), ((), ())), preferred_element_type=F32)


INPROJ_SUB_ROWS = 256


def _inproj_kernel(x_ref, nw_ref, w_ref, proj_ref, *win_refs, win_first_tile, tiles_per_seq):
    tile = x_ref.shape[0]
    sub = min(tile, INPROJ_SUB_ROWS)
    hs = [_rmsnorm(x_ref[i * sub:(i + 1) * sub, :], nw_ref[...]).astype(BF16) for i in range(tile // sub)]
    groups = []
    for h in hs:
        for g in range(N_GROUPS):
            r = _dot(h, w_ref[:, g * D_HEADS:(g + 1) * D_HEADS])
            if g in (1, 4):
                r = r * (HEAD_DIM ** -0.5)
            groups.append(r)
    for i in range(tile // sub):
        for g in range(N_GROUPS):
            proj_ref[i * sub:(i + 1) * sub, g * D_HEADS:(g + 1) * D_HEADS] = groups[i * N_GROUPS + g].astype(
                proj_ref.dtype)
    if win_refs:
        @pl.when(pl.program_id(0) % tiles_per_seq >= win_first_tile)
        def _():
            for i in range(tile // sub):
                for out, r in zip(win_refs, groups[i * N_GROUPS + 5:i * N_GROUPS + 7]):
                    out[0, :, i * sub:(i + 1) * sub] = r.T


def _inproj(x2d, norm_w, w_bf16, *, tile, out_dtype, seq_len=None, win_rows=None):
    n = x2d.shape[0]
    assert n % tile == 0
    in_specs = [
        pl.BlockSpec((tile, D_MODEL), lambda i: (i, 0)),
        _const_spec((1, D_MODEL)),
        _const_spec((D_MODEL, D_IN)),
    ]
    out_shape = [jax.ShapeDtypeStruct((n, D_IN), out_dtype)]
    out_specs = [pl.BlockSpec((tile, D_IN), lambda i: (i, 0))]
    kw = dict(win_first_tile=0, tiles_per_seq=1)
    if win_rows is not None:
        assert seq_len % tile == 0 and win_rows % tile == 0
        tps = seq_len // tile
        first = (seq_len - win_rows) // tile
        kw = dict(win_first_tile=first, tiles_per_seq=tps)

        def win_map(i):
            return (i // tps, 0, jnp.maximum(i % tps - first, 0))

        for _ in range(2):
            out_shape.append(jax.ShapeDtypeStruct((n // seq_len, D_HEADS, win_rows), F32))
            out_specs.append(pl.BlockSpec((1, D_HEADS, tile), win_map))
    return pl.pallas_call(
        functools.partial(_inproj_kernel, **kw),
        grid=(n // tile,),
        in_specs=in_specs,
        out_specs=out_specs,
        out_shape=out_shape,
        compiler_params=_params(("arbitrary",)),
        name="inproj",
    )(x2d, norm_w.reshape(1, D_MODEL), w_bf16)


def _retention_tables(chunk):
    lg = np.log1p(-np.exp2(-5.0 - np.arange(N_HEADS, dtype=np.float64)))
    i = np.arange(chunk, dtype=np.float64)
    diff = i[:, None] - i[None, :]
    intra = np.where(diff >= 0, np.exp(lg[:, None, None] * np.maximum(diff, 0.0)), 0.0)
    lane_head = np.arange(D_HEADS) // HEAD_DIM
    q_dec = np.exp(lg[lane_head][None, :] * (i[:, None] + 1.0))
    k_dec = np.exp(lg[lane_head][None, :] * (chunk - 1.0 - i[:, None]))
    c_dec = np.exp(lg * chunk)
    row_head = np.arange(PAIR) // HEAD_DIM
    same = (row_head[:, None] == row_head[None, :]).astype(np.float64)
    c_pair = c_dec.reshape(N_PAIRS, 2)[:, row_head][:, :, None] * same[None]
    return tuple(a.astype(np.float32) for a in (intra, q_dec, k_dec, c_pair, same))


def _retention_init(*refs):
    state_ref = refs[-1]

    @pl.when(pl.program_id(0) == 0)
    def _():
        state_ref[...] = jnp.zeros(state_ref.shape, state_ref.dtype)


def _retention_kernel(q_ref, k_ref, v_ref, g_ref, s0_ref, intra_ref, qdec_ref, kdec_ref, cpair_ref, same_ref,
                      gw_ref, gb_ref, ro_ref, sout_ref, state_ref, *, chunk, chunks_per_tile, tiles_per_seq):
    n_pairs = N_PAIRS
    first_tile = pl.program_id(0) % tiles_per_seq == 0
    left = lax.broadcasted_iota(jnp.int32, (chunk, PAIR), 1) < HEAD_DIM

    averager = (same_ref[...] * (1.0 / HEAD_DIM)).astype(BF16)

    def head_mean(a):
        hi = a.astype(BF16)
        lo = (a - hi.astype(F32)).astype(BF16)
        return _dot(hi, averager) + _dot(lo, averager)

    items = [(c, p) for p in range(n_pairs) for c in range(chunks_per_tile)]
    loaded = {}
    for c, p in items:
        rows = slice(c * chunk, (c + 1) * chunk)
        loaded[c, p] = tuple(ref[0, rows, p * PAIR:(p + 1) * PAIR] for ref in (q_ref, k_ref, v_ref, g_ref))
    kept = [state_ref[p] for p in range(n_pairs)]
    zero_block = jnp.zeros((HEAD_DIM, HEAD_DIM), F32)

    def pair_state(p):
        top = jnp.concatenate([s0_ref[0, 2 * p], zero_block], axis=1)
        bottom = jnp.concatenate([zero_block, s0_ref[0, 2 * p + 1]], axis=1)
        return jnp.concatenate([top, bottom], axis=0)

    states = [jnp.where(first_tile, pair_state(p), kept[p]) for p in range(n_pairs)]
    lanes_of = {p: slice(p * PAIR, (p + 1) * PAIR) for p in range(n_pairs)}
    q2 = {cp: loaded[cp][0].astype(BF16) for cp in items}
    k2 = {cp: loaded[cp][1].astype(BF16) for cp in items}
    v2 = {cp: loaded[cp][2].astype(BF16) for cp in items}
    raw = {}
    for cp in items:
        for hh in range(2):
            mine = left if hh == 0 else jnp.logical_not(left)
            raw[cp, hh] = _dot_nt(jnp.where(mine, q2[cp], jnp.zeros_like(q2[cp])), k2[cp])
    yield
    sc = {(cp, hh): (raw[cp, hh] * intra_ref[2 * cp[1] + hh]).astype(BF16) for cp in items for hh in range(2)}
    kd = {cp: (k2[cp].astype(F32) * kdec_ref[:, lanes_of[cp[1]]]).astype(BF16) for cp in items}
    yield
    intra_out = {cp: jnp.where(left, _dot(sc[cp, 0], v2[cp]), _dot(sc[cp, 1], v2[cp])) for cp in items}
    yield
    kv = {cp: _dot_tn(kd[cp], v2[cp]) * same_ref[...] for cp in items}
    yield
    state_at = {}
    for c, p in items:
        state_at[c, p] = states[p]
        states[p] = states[p] * cpair_ref[p] + kv[c, p]
    o = {cp: intra_out[cp] + _dot(q2[cp], state_at[cp].astype(BF16)) * qdec_ref[:, lanes_of[cp[1]]] for cp in items}
    yield
    mean = {cp: head_mean(o[cp]) for cp in items}
    dlt = {cp: o[cp] - mean[cp] for cp in items}
    var = {cp: head_mean(dlt[cp] * dlt[cp]) for cp in items}
    yield
    outs = {}
    for cp in items:
        lanes = lanes_of[cp[1]]
        y = dlt[cp] * lax.rsqrt(var[cp] + EPS) * gw_ref[:, lanes] + gb_ref[:, lanes]
        outs[cp] = (_silu(loaded[cp][3].astype(F32)) * y).astype(ro_ref.dtype)
    yield
    for c, p in items:
        ro_ref[0, c * chunk:(c + 1) * chunk, lanes_of[p]] = outs[c, p]
    for p in range(n_pairs):
        state_ref[p] = states[p]
        sout_ref[0, 2 * p] = states[p][:HEAD_DIM, :HEAD_DIM]
        sout_ref[0, 2 * p + 1] = states[p][HEAD_DIM:, HEAD_DIM:]


class _CallParts:
    def __init__(self, body, pieces, steps, operands, in_specs, out_specs, out_shape, scratch_shapes=(), init=None):
        self.body, self.pieces, self.init, self.steps, self.operands = body, pieces, init, steps, list(operands)
        self.in_specs, self.out_specs, self.out_shape = list(in_specs), list(out_specs), list(out_shape)
        self.scratch_shapes = list(scratch_shapes)

    @property
    def n_refs(self):
        return len(self.operands), len(self.out_shape), len(self.scratch_shapes)


def _run_parts(parts, name):
    steps = parts[0].steps
    assert all(p.steps == steps for p in parts)

    def fused(*refs):
        pos = 0
        groups = []
        for kind in range(3):
            per_part = []
            for p in parts:
                n = p.n_refs[kind]
                per_part.append(refs[pos:pos + n])
                pos += n
            groups.append(per_part)
        for i, p in enumerate(parts):
            if p.init is not None:
                p.init(*groups[0][i], *groups[1][i], *groups[2][i])
        gens = [p.body(*groups[0][i], *groups[1][i], *groups[2][i]) for i, p in enumerate(parts)]
        slots = sorted(((k + 0.5) / p.pieces, i) for i, p in enumerate(parts) for k in range(p.pieces))
        done = object()
        for _, i in slots:
            next(gens[i], done)
        for g in gens:
            assert next(g, done) is done, "a part yielded more pieces than it declared"

    outs = pl.pallas_call(
        fused,
        grid=(steps,),
        in_specs=[s for p in parts for s in p.in_specs],
        out_specs=[s for p in parts for s in p.out_specs],
        out_shape=[s for p in parts for s in p.out_shape],
        scratch_shapes=[s for p in parts for s in p.scratch_shapes],
        compiler_params=_params(("arbitrary",)),
        name=name,
    )(*[o for p in parts for o in p.operands])
    split, pos = [], 0
    for p in parts:
        split.append(outs[pos:pos + len(p.out_shape)])
        pos += len(p.out_shape)
    return split


def _retention_parts(proj, s0, gn_w, gb_b, *, tile):
    b, t, _ = proj.shape
    chunk = math.gcd(t, RET_CHUNK)
    assert t % tile == 0 and tile % chunk == 0
    tps = t // tile

    def grp_spec(g):
        return pl.BlockSpec((1, tile, D_HEADS), lambda i: (i // tps, i % tps, g))

    state_spec = pl.BlockSpec((1, N_HEADS, HEAD_DIM, HEAD_DIM), lambda i: (i // tps, 0, 0, 0))
    return _CallParts(
        functools.partial(_retention_kernel, chunk=chunk, chunks_per_tile=tile // chunk, tiles_per_seq=tps),
        8,
        b * tps,
        (proj, proj, proj, proj, s0.astype(F32), *_retention_tables(chunk),
         gn_w.reshape(1, D_HEADS).astype(F32), gb_b.reshape(1, D_HEADS).astype(F32)),
        in_specs=[
            grp_spec(0), grp_spec(1), grp_spec(2), grp_spec(3),
            state_spec,
            _const_spec((N_HEADS, chunk, chunk)),
            _const_spec((chunk, D_HEADS)),
            _const_spec((chunk, D_HEADS)),
            _const_spec((N_PAIRS, PAIR, PAIR)),
            _const_spec((PAIR, PAIR)),
            _const_spec((1, D_HEADS)),
            _const_spec((1, D_HEADS)),
        ],
        out_specs=[
            pl.BlockSpec((1, tile, D_HEADS), lambda i: (i // tps, i % tps, 0)),
            state_spec,
        ],
        out_shape=[
            jax.ShapeDtypeStruct((b, t, D_HEADS), BF16),
            jax.ShapeDtypeStruct((b, N_HEADS, HEAD_DIM, HEAD_DIM), F32),
        ],
        scratch_shapes=[pltpu.VMEM((N_PAIRS, PAIR, PAIR), F32)],
        init=_retention_init,
    )


def _alibi_slopes():
    return np.exp2(-8.0 * np.arange(1, N_HEADS + 1, dtype=np.float64) / N_HEADS)


def _prompt_attn_kernel(q_ref, k_ref, v_ref, slope_ref, o_ref, xf_ref, xg_ref, qp_ref, kp_ref, vp_ref, st_ref, *, seq):
    hp = pl.program_id(1)
    left = lax.broadcasted_iota(jnp.int32, (Q_TILE, PAIR), 1) < HEAD_DIM
    ones = jnp.ones((2 * Q_TILE, PAIR), BF16)

    xf_ref[0] = q_ref[0].astype(F32)
    xf_ref[1] = k_ref[0].astype(F32)
    xf_ref[2] = v_ref[0].astype(F32)
    f32_bufs = (xf_ref, xg_ref)

    def run_group(tiles, biases, state_idx, first_branch, last_branch, qs, ks, vs):
        ops = [(qs[pl.ds(q0, Q_TILE), :], ks[pl.ds(k0, nk), :], vs[pl.ds(k0, nk), :]) for q0, k0, nk in tiles]
        old = None if first_branch else [tuple(st_ref[c, idx, :] for c in range(3)) for idx in state_idx]
        scores = []
        for (q, k, _), bias in zip(ops, biases):
            for hh in range(2):
                mine = left if hh == 0 else jnp.logical_not(left)
                scores.append(_dot_nt(jnp.where(mine, q, jnp.zeros_like(q)), k) + bias[hh])
        probs = []
        for s in scores:
            m = jnp.max(s, axis=-1, keepdims=True)
            probs.append((m, jnp.exp(s - m).astype(BF16)))
        new = []
        for ti, (_, _, v) in enumerate(ops):
            vaug = jnp.concatenate([v, ones[:v.shape[0]]], axis=1)
            res = []
            for hh in range(2):
                m, p = probs[2 * ti + hh]
                pv = _dot(p, vaug)
                res.append((jnp.broadcast_to(m, (Q_TILE, PAIR)), pv[:, PAIR:], pv[:, :PAIR]))
            m, den, num = (jnp.where(left, a, b) for a, b in zip(res[0], res[1]))
            if first_branch:
                new.append((m, den, num))
            else:
                m0, den0, num0 = old[ti]
                mm = jnp.maximum(m0, m)
                w0 = jnp.exp(m0 - mm)
                w1 = jnp.exp(m - mm)
                den1, num1 = den0 * w0 + den * w1, num0 * w0 + num * w1
                new.append((num1 / den1,) if last_branch else (mm, den1, num1))
        for idx, vals in zip(state_idx, new):
            for c, val in enumerate(vals):
                st_ref[c, idx, :] = val

    def strided_pass(bi, write_f32, write_bf16):
        d = BRANCHES[bi][1]
        assert d == BRANCHES[bi - 1][1] * DIL_STEP
        seg = seq // d
        for s_prev in range(d // DIL_STEP):
            for u in range(DIL_STEP):
                rows = slice((s_prev * DIL_STEP + u) * seg, (s_prev * DIL_STEP + u + 1) * seg)
                for a, out in enumerate((qp_ref, kp_ref, vp_ref)):
                    val = f32_bufs[bi - 1][a, pl.ds(s_prev * seg * DIL_STEP + u, seg, stride=DIL_STEP), :]
                    if write_bf16:
                        out[rows, :] = val.astype(BF16)
                    if write_f32:
                        f32_bufs[bi][a, rows, :] = val

    n_br = len(BRANCHES)
    assert 2 <= n_br <= len(f32_bufs) + 1
    for bi in range(1, n_br - 1):
        strided_pass(bi, True, False)

    for step, bi in enumerate(reversed(range(n_br))):
        window, d = BRANCHES[bi]
        assert window // d == Q_TILE
        seg = seq // d
        tiles_per_seg = seg // Q_TILE
        segs_per_iter = max(1, ATTN_GROUPS[bi] // tiles_per_seg)
        group = min(ATTN_GROUPS[bi], tiles_per_seg)
        assert tiles_per_seg % group == 0
        qi = lax.broadcasted_iota(jnp.int32, (Q_TILE, 2 * Q_TILE), 0)
        kj = lax.broadcasted_iota(jnp.int32, (Q_TILE, 2 * Q_TILE), 1)
        n_rest = qi + Q_TILE - kj
        n_first = n_rest[:, :Q_TILE] - Q_TILE
        bias_rest, bias_first = [], []
        for hh in range(2):
            sl = slope_ref[pl.ds(2 * hp + hh, 1), :]
            sl2 = jnp.concatenate([sl, sl], axis=1)
            pen_rest = sl2 * (d * n_rest).astype(F32)
            pen_first = sl * (d * n_first).astype(F32)
            bias_rest.append(jnp.where((n_rest >= 0) & (n_rest <= Q_TILE), -pen_rest, NEG_INF))
            bias_first.append(jnp.where(n_first >= 0, -pen_first, NEG_INF))

        if bi == 0:
            assert d == 1
            qs, ks, vs = q_ref.at[0], k_ref.at[0], v_ref.at[0]
        else:
            qs, ks, vs = qp_ref, kp_ref, vp_ref
            if bi == n_br - 1:
                strided_pass(bi, False, True)
            else:
                for a, out in enumerate((qp_ref, kp_ref, vp_ref)):
                    out[...] = f32_bufs[bi][a].astype(BF16)

        def seg_body(r0, carry, *, d=d, seg=seg, tiles_per_seg=tiles_per_seg, group=group,
                     segs_per_iter=segs_per_iter, qs=qs, ks=ks, vs=vs,
                     bias_first=bias_first, bias_rest=bias_rest, first_branch=(step == 0),
                     last_branch=(step == n_br - 1)):
            def residue(s, dil):
                if dil == 1:
                    return 0
                return (dil // DIL_STEP) * (s % DIL_STEP) + residue(s // DIL_STEP, dil // DIL_STEP)

            def tile_spec(s, tile_in_seg, is_first):
                q0 = pl.multiple_of(s * seg + tile_in_seg * Q_TILE, Q_TILE)
                start = d * Q_TILE * tile_in_seg + residue(s, d)
                rows = pl.ds(start, Q_TILE) if d == 1 else pl.ds(start, Q_TILE, stride=d)
                if is_first:
                    return (q0, q0, Q_TILE), bias_first, rows
                return (q0, q0 - Q_TILE, 2 * Q_TILE), bias_rest, rows

            def run(specs):
                tiles, biases, rows = zip(*specs)
                run_group(tiles, biases, rows, first_branch, last_branch, qs, ks, vs)

            segs = [r0 * segs_per_iter + i for i in range(segs_per_iter)]
            run([tile_spec(r, u, u == 0) for r in segs for u in range(group)])

            def group_body(g, carry2):
                run([tile_spec(segs[0], g * group + u, False) for u in range(group)])
                return carry2

            if tiles_per_seg > group:
                carry = lax.fori_loop(1, tiles_per_seg // group, group_body, carry)
            return carry

        lax.fori_loop(0, d // segs_per_iter, seg_body, 0)

    assert len(BRANCHES) >= 2
    o_ref[0] = st_ref[0].astype(o_ref.dtype)


def _prompt_attn(proj):
    b, t, _ = proj.shape
    assert t % (BRANCHES[-1][1] * Q_TILE) == 0
    slopes = np.broadcast_to(_alibi_slopes()[:, None], (N_HEADS, PAIR)).astype(np.float32)

    def col_spec(group):
        return pl.BlockSpec((1, t, PAIR), lambda i, j: (i, 0, group * N_PAIRS + j))

    return pl.pallas_call(
        functools.partial(_prompt_attn_kernel, seq=t),
        grid=(b, N_PAIRS),
        in_specs=[col_spec(4), col_spec(5), col_spec(6), _const_spec((N_HEADS, PAIR))],
        out_specs=pl.BlockSpec((1, t, PAIR), lambda i, j: (i, 0, j)),
        out_shape=jax.ShapeDtypeStruct((b, t, D_HEADS), BF16),
        scratch_shapes=[
            pltpu.VMEM((3, t, PAIR), F32),
            pltpu.VMEM((3, t, PAIR), F32),
            pltpu.VMEM((t, PAIR), BF16),
            pltpu.VMEM((t, PAIR), BF16),
            pltpu.VMEM((t, PAIR), BF16),
            pltpu.VMEM((3, t, PAIR), F32),
        ],
        compiler_params=_params(("arbitrary", "arbitrary")),
        name="prompt_attn",
    )(proj, proj, proj, slopes)


def _sample_tables(n_new, n_cache):
    slopes = _alibi_slopes()
    qpos = n_cache + np.arange(n_new)
    kpos = np.arange(n_cache + n_new)
    dist = qpos[:, None] - kpos[None, :]
    cnt = np.zeros(dist.shape, np.float32)
    for window, d in BRANCHES:
        cnt = cnt + ((dist >= 0) & (dist <= window) & (dist % d == 0)).astype(np.float32)
    bias = -slopes[:, None, None] * dist.astype(np.float64)[None]
    bias = np.where(cnt[None] > 0, bias, NEG_INF).reshape(N_HEADS * n_new, -1).astype(np.float32)
    return bias[:, :n_cache], cnt[:, :n_cache], bias[:, n_cache:], cnt[:, n_cache:]


def _sample_attn_kernel(q_ref, kn_ref, vn_ref, knt_ref, vnt_ref, ck_ref, cv_ref, bc_ref, cc_ref, bn_ref, cn_ref,
                        o_ref, wk_ref, wv_ref, *, n_new, n_cache):
    keep = n_cache - n_new
    heads = range(ck_ref.shape[1])
    lanes = [slice(h * HEAD_DIM, (h + 1) * HEAD_DIM) for h in heads]
    rows = [slice(h * n_new, (h + 1) * n_new) for h in heads]
    q = [q_ref[0, :, lanes[h]].astype(BF16) for h in heads]
    k_new = [kn_ref[0, :, lanes[h]].astype(BF16) for h in heads]
    v_new = [vn_ref[0, :, lanes[h]].astype(BF16) for h in heads]
    s_c = [_dot(q[h], ck_ref[0, h].astype(BF16)) + bc_ref[rows[h], :] for h in heads]
    s_n = [_dot_nt(q[h], k_new[h]) + bn_ref[rows[h], :] for h in heads]
    yield
    m = [jnp.maximum(jnp.max(s_c[h], axis=-1, keepdims=True), jnp.max(s_n[h], axis=-1, keepdims=True))
         for h in heads]
    p_c = [cc_ref[...] * jnp.exp(s_c[h] - m[h]) for h in heads]
    p_n = [cn_ref[...] * jnp.exp(s_n[h] - m[h]) for h in heads]
    den = [jnp.sum(p_c[h], axis=-1, keepdims=True) + jnp.sum(p_n[h], axis=-1, keepdims=True) for h in heads]
    yield
    o = [_dot_nt(p_c[h].astype(BF16), cv_ref[0, h].astype(BF16)) + _dot(p_n[h].astype(BF16), v_new[h])
         for h in heads]
    yield
    for h in heads:
        wk_ref[0, h] = pltpu.roll(ck_ref[0, h], keep, 1)
        wv_ref[0, h] = pltpu.roll(cv_ref[0, h], keep, 1)
        wk_ref[0, h, :, keep:n_cache] = knt_ref[0, lanes[h], :]
        wv_ref[0, h, :, keep:n_cache] = vnt_ref[0, lanes[h], :]
        o_ref[0, :, lanes[h]] = (o[h] / den[h]).astype(o_ref.dtype)
        yield


def _sample_attn_parts(proj, knt, vnt, cache_kt, cache_vt):
    b, n_new, _ = proj.shape
    n_cache = cache_kt.shape[-1]
    assert n_cache == MAX_WINDOW and n_new % 8 == 0
    rows = N_HEADS * n_new
    tables = _sample_tables(n_new, n_cache)

    def grp_spec(group):
        return pl.BlockSpec((1, n_new, D_HEADS), lambda i: (i, 0, group))

    new_t_spec = pl.BlockSpec((1, D_HEADS, n_new), lambda i: (i, 0, 0))
    cache_spec = pl.BlockSpec((1, N_HEADS, HEAD_DIM, n_cache), lambda i: (i, 0, 0, 0))
    return _CallParts(
        functools.partial(_sample_attn_kernel, n_new=n_new, n_cache=n_cache),
        N_HEADS + 4,
        b,
        (proj, proj, proj, knt, vnt, cache_kt, cache_vt, *tables),
        in_specs=[grp_spec(4), grp_spec(5), grp_spec(6), new_t_spec, new_t_spec, cache_spec, cache_spec,
                  _const_spec((rows, n_cache)), _const_spec((n_new, n_cache)),
                  _const_spec((rows, n_new)), _const_spec((n_new, n_new))],
        out_specs=[pl.BlockSpec((1, n_new, D_HEADS), lambda i: (i, 0, 0)), cache_spec, cache_spec],
        out_shape=[
            jax.ShapeDtypeStruct((b, n_new, D_HEADS), BF16),
            jax.ShapeDtypeStruct((b, N_HEADS, HEAD_DIM, n_cache), F32),
            jax.ShapeDtypeStruct((b, N_HEADS, HEAD_DIM, n_cache), F32),
        ],
    )


FF_CHUNK = 1408
N_FF_CHUNKS = D_FF // FF_CHUNK
HALO = 8
FFN_SUB_ROWS = 512


def _mix_ffn_kernel(x_ref, ro_ref, ao_ref, wo_ref, n2_ref, wup_ref, cw_ref, cb_ref, wdn_ref, nf_ref, prev_ref,
                    y_ref, ua_ref, halo_ref, *, tile, seq_in_tile, tiles_per_seq):
    t = pl.program_id(0) % tiles_per_seq
    sub = min(tile, FFN_SUB_ROWS)
    chunk_cols = [slice(c * FF_CHUNK, (c + 1) * FF_CHUNK) for c in range(N_FF_CHUNKS)]
    if seq_in_tile is None:
        halo = jnp.where(t == 0, prev_ref[0], halo_ref[...])
        halos = [halo[:, cols] for cols in chunk_cols]
        head_row = lax.broadcasted_iota(jnp.int32, (HALO, FF_CHUNK), 0)
    else:
        assert sub == tile
        pos = lax.broadcasted_iota(jnp.int32, (tile, FF_CHUNK), 0) % seq_in_tile
    ys, uas = [], None
    for i in range(tile // sub):
        rows = slice(i * sub, (i + 1) * sub)
        x1 = (x_ref[0, rows, :] + _dot(ro_ref[0, rows, :], wo_ref[0:D_HEADS, :])
              + _dot(ao_ref[0, rows, :], wo_ref[D_HEADS:2 * D_HEADS, :]))
        h2 = _rmsnorm(x1, n2_ref[...]).astype(BF16)
        yield
        gates, uas = [], []
        for c, cols in enumerate(chunk_cols):
            ua = _dot(h2, wup_ref[:, cols])
            yield
            ub = _dot(h2, wup_ref[:, D_FF + c * FF_CHUNK:D_FF + (c + 1) * FF_CHUNK])
            yield
            conv = cb_ref[:, cols]
            for j in range(CONV_W - 1):
                back = CONV_W - 1 - j
                tap = pltpu.roll(ua, back, 0)
                if seq_in_tile is None:
                    head = jnp.where(head_row < back, pltpu.roll(halos[c], back, 0), tap[0:HALO])
                    tap = jnp.concatenate([head, tap[HALO:]], axis=0)
                else:
                    tap = jnp.where(pos < back, prev_ref[j, :, cols], tap)
                conv = conv + tap * cw_ref[j:j + 1, cols]
            conv = conv + ua * cw_ref[CONV_W - 1:CONV_W, cols]
            gates.append((_silu(conv) * ub).astype(BF16))
            uas.append(ua)
            if seq_in_tile is None:
                halos[c] = ua[sub - HALO:sub]
            yield
        ys.append(_rmsnorm(x1 + _dot(jnp.concatenate(gates, axis=1), wdn_ref[...]), nf_ref[...]))
        yield
    for i, y in enumerate(ys):
        y_ref[0, i * sub:(i + 1) * sub, :] = y
    for c, cols in enumerate(chunk_cols):
        if seq_in_tile is None:
            ua_ref[0, :, cols] = halos[c]
            halo_ref[:, cols] = halos[c]
        else:
            ua_ref[0, :, cols] = uas[c]


def _mix_ffn_parts(x, ro, ao, w_out, norm2_w, w_up, conv_w, conv_b, w_down, normf_w, prev, *, tile, seq_in_tile):
    b, t, _ = x.shape
    assert t % tile == 0
    tps = t // tile
    if seq_in_tile is None:
        prev_spec = pl.BlockSpec((1, HALO, D_FF), lambda i: (i // tps, 0, 0))
        ua_rows = HALO
    else:
        assert b == 1 and t == tile
        prev_spec = _const_spec((CONV_W - 1, tile, D_FF))
        ua_rows = tile

    def row_spec(width):
        return pl.BlockSpec((1, tile, width), lambda i: (i // tps, i % tps, 0))

    return _CallParts(
        functools.partial(_mix_ffn_kernel, tile=tile, seq_in_tile=seq_in_tile, tiles_per_seq=tps),
        (tile // min(tile, FFN_SUB_ROWS)) * (2 + 3 * N_FF_CHUNKS) + 1,
        b * tps,
        (x, ro, ao, w_out, norm2_w.reshape(1, D_MODEL), w_up, conv_w, conv_b.reshape(1, D_FF), w_down,
         normf_w.reshape(1, D_MODEL), prev),
        in_specs=[
            row_spec(D_MODEL),
            row_spec(D_HEADS),
            row_spec(D_HEADS),
            _const_spec((2 * D_HEADS, D_MODEL)),
            _const_spec((1, D_MODEL)),
            _const_spec((D_MODEL, 2 * D_FF)),
            _const_spec((CONV_W, D_FF)),
            _const_spec((1, D_FF)),
            _const_spec((D_FF, D_MODEL)),
            _const_spec((1, D_MODEL)),
            prev_spec,
        ],
        out_specs=[
            row_spec(D_MODEL),
            pl.BlockSpec((1, ua_rows, D_FF), lambda i: (i // tps, 0, 0)),
        ],
        out_shape=[
            jax.ShapeDtypeStruct((b, t, D_MODEL), F32),
            jax.ShapeDtypeStruct((b, ua_rows, D_FF), F32),
        ],
        scratch_shapes=[pltpu.VMEM((HALO, D_FF), F32)],
    )


INPROJ_ROW_TILE = 1024
RET_ROW_TILE = 512
FFN_ROW_TILE = 512


def _rows_to_lanes(cache):
    return cache.transpose(0, 2, 3, 1)


def _lanes_to_rows(cache_t):
    return cache_t.transpose(0, 3, 1, 2)


def kernel(x_prompt, x_sample, state_ret, cache_win_k, cache_win_v, state_conv, norm1_w, w_in, ret_gn_w, ret_gn_b,
           w_out, norm2_w, w_up, conv_w, conv_b, w_down, normf_w):
    bp, tp, _ = x_prompt.shape
    bs, ts, _ = x_sample.shape
    w_in_b, w_out_b, w_up_b, w_down_b = (w.astype(BF16) for w in (w_in, w_out, w_up, w_down))
    ffn_w = (w_out_b, norm2_w, w_up_b, conv_w, conv_b, w_down_b, normf_w)

    win = min(MAX_WINDOW, tp)
    proj_p, wk_p, wv_p = _inproj(x_prompt.reshape(bp * tp, D_MODEL), norm1_w, w_in_b, tile=INPROJ_ROW_TILE,
                                 out_dtype=BF16, seq_len=tp, win_rows=win)
    proj_p = proj_p.reshape(bp, tp, D_IN)
    proj_s, knt, vnt = _inproj(x_sample.reshape(bs * ts, D_MODEL), norm1_w, w_in_b, tile=bs * ts, out_dtype=F32,
                               seq_len=bs * ts, win_rows=bs * ts)
    proj_s = proj_s.reshape(bs, ts, D_IN)
    knt, vnt = (a.reshape(D_HEADS, bs, ts).transpose(1, 0, 2) for a in (knt, vnt))

    ao_p = _prompt_attn(proj_p)

    ret_p_parts = _retention_parts(proj_p, jnp.zeros((bp, N_HEADS, HEAD_DIM, HEAD_DIM), F32), ret_gn_w, ret_gn_b,
                                   tile=RET_ROW_TILE)
    smp_parts = _sample_attn_parts(proj_s, knt, vnt, _rows_to_lanes(cache_win_k), _rows_to_lanes(cache_win_v))
    ret_s_parts = _retention_parts(proj_s, state_ret, ret_gn_w, ret_gn_b, tile=ts)
    if ret_p_parts.steps == smp_parts.steps == ret_s_parts.steps:
        (ro_p, ret_p), (ao_s, wk_s, wv_s), (ro_s, ret_s) = _run_parts(
            [ret_p_parts, smp_parts, ret_s_parts], "retention_and_cache")
    else:
        (ro_p, ret_p), = _run_parts([ret_p_parts], "retention")
        (ao_s, wk_s, wv_s), = _run_parts([smp_parts], "sample_attn")
        (ro_s, ret_s), = _run_parts([ret_s_parts], "retention")
    (y_p, ua_tail_p), = _run_parts(
        [_mix_ffn_parts(x_prompt, ro_p, ao_p, *ffn_w, jnp.zeros((bp, HALO, D_FF), F32), tile=FFN_ROW_TILE,
                        seq_in_tile=None)], "mix_ffn")
    conv_p = ua_tail_p[:, HALO - (CONV_W - 1):]

    pad = jnp.zeros((bs, ts - (CONV_W - 1), D_FF), F32)
    prev_s = jnp.stack([
        jnp.concatenate([state_conv[:, j:], pad, jnp.zeros((bs, j, D_FF), F32)], axis=1).reshape(bs * ts, D_FF)
        for j in range(CONV_W - 1)])
    (y_s, ua_s), = _run_parts(
        [_mix_ffn_parts(x_sample.reshape(1, bs * ts, D_MODEL), ro_s.reshape(1, bs * ts, D_HEADS),
                        ao_s.reshape(1, bs * ts, D_HEADS), *ffn_w, prev_s, tile=bs * ts, seq_in_tile=ts)],
        "mix_ffn")
    conv_s = ua_s.reshape(bs, ts, D_FF)[:, ts - (CONV_W - 1):]

    return (y_p, y_s.reshape(bs, ts, D_MODEL),
            ret_p.astype(x_prompt.dtype), ret_s.astype(state_ret.dtype),
            _lanes_to_rows(wk_p.reshape(bp, N_HEADS, HEAD_DIM, win)),
            _lanes_to_rows(wv_p.reshape(bp, N_HEADS, HEAD_DIM, win)),
            _lanes_to_rows(wk_s), _lanes_to_rows(wv_s),
            conv_p, conv_s)
```

```python
import functools
import math

import jax
import jax.numpy as jnp
import numpy as np
from jax import lax
from jax.experimental import pallas as pl
from jax.experimental.pallas import tpu as pltpu

F32 = jnp.float32
BF16 = jnp.bfloat16

D_MODEL = 1024
N_HEADS = 8
HEAD_DIM = 64
D_HEADS = N_HEADS * HEAD_DIM
N_GROUPS = 7
D_IN = N_GROUPS * D_HEADS
PAIR = 2 * HEAD_DIM
N_PAIRS = N_HEADS // 2
RET_CHUNK = 128
BRANCHES = ((128, 1), (512, 4), (2048, 16))
MAX_WINDOW = 2048
Q_TILE = 128
DIL_STEP = 4
ATTN_GROUPS = (8, 4, 8)
D_FF = 2816
CONV_W = 3
EPS = 1e-6
NEG_INF = float("-inf")

V7X_VMEM_LIMIT = 56 * 1024 * 1024


def _params(sem, vmem=V7X_VMEM_LIMIT):
    return pltpu.CompilerParams(dimension_semantics=sem, vmem_limit_bytes=vmem)


def _const_spec(shape):
    nd = len(shape)
    return pl.BlockSpec(shape, lambda *_: (0,) * nd, pipeline_mode=pl.Buffered(1))


def _rmsnorm(x, w):
    return x * lax.rsqrt(jnp.mean(x * x, axis=-1, keepdims=True) + EPS) * w


def _silu(x):
    return x * jax.nn.sigmoid(x)


def _dot(a, b):
    return jnp.dot(a, b, preferred_element_type=F32)


def _dot_nt(a, b):
    return lax.dot_general(a, b, (((1,), (1,)), ((), ())), preferred_element_type=F32)


def _dot_tn(a, b):
    return lax.dot_general(a, b, (((0,), (0,)), ((), ())), preferred_element_type=F32)


INPROJ_SUB_ROWS = 256


def _inproj_kernel(x_ref, nw_ref, w_ref, proj_ref, *rest, win_first_tile, tiles_per_seq, n_win, n_perm):
    win_refs, perm_refs = rest[:n_win], rest[n_win:n_win + n_perm]
    tile = x_ref.shape[0]
    sub = min(tile, INPROJ_SUB_ROWS)
    hs = [_rmsnorm(x_ref[i * sub:(i + 1) * sub, :], nw_ref[...]).astype(BF16) for i in range(tile // sub)]
    def project(h, g):
        r = _dot(h, w_ref[:, g * D_HEADS:(g + 1) * D_HEADS])
        if g in (1, 4):
            r = r * (HEAD_DIM ** -0.5)
        return r

    def reorder(i, qkv):
        stage_ref = rest[n_win + n_perm]
        slabs = [(a, lane) for a in range(3) for lane in range(D_HEADS // PAIR)]
        for n, (a, lane) in enumerate(slabs):
            stage_ref[0, i, n] = qkv[a][:, lane * PAIR:(lane + 1) * PAIR]
        for level, out in enumerate(perm_refs, start=1):
            segs_src = DIL_STEP ** (level - 1)
            rows_src = sub // segs_src
            rows_dst = rows_src // DIL_STEP
            for n, (a, lane) in enumerate(slabs):
                cols = slice(a * D_HEADS + lane * PAIR, a * D_HEADS + (lane + 1) * PAIR)
                for s_prev in range(segs_src):
                    for u in range(DIL_STEP):
                        s = s_prev * DIL_STEP + u
                        val = stage_ref[level - 1, i, n, pl.ds(s_prev * rows_src + u, rows_dst, stride=DIL_STEP), :]
                        out[0, s, i * rows_dst:(i + 1) * rows_dst, cols] = val.astype(out.dtype)
                        if level < n_perm:
                            stage_ref[level, i, n, s * rows_dst:(s + 1) * rows_dst, :] = val

    groups = {}
    for i, h in enumerate(hs):
        for g in (4, 5, 6):
            groups[i, g] = project(h, g)
        if perm_refs:
            reorder(i, [groups[i, g] for g in (4, 5, 6)])
        for g in (0, 1, 2, 3):
            groups[i, g] = project(h, g)
        for g in range(N_GROUPS):
            proj_ref[i * sub:(i + 1) * sub, g * D_HEADS:(g + 1) * D_HEADS] = groups[i, g].astype(proj_ref.dtype)
    if win_refs:
        @pl.when(pl.program_id(0) % tiles_per_seq >= win_first_tile)
        def _():
            for i in range(tile // sub):
                for out, g in zip(win_refs, (5, 6)):
                    out[0, :, i * sub:(i + 1) * sub] = groups[i, g].T


def _inproj(x2d, norm_w, w_bf16, *, tile, out_dtype, seq_len=None, win_rows=None, perm_dilations=()):
    n = x2d.shape[0]
    assert n % tile == 0
    in_specs = [
        pl.BlockSpec((tile, D_MODEL), lambda i: (i, 0)),
        _const_spec((1, D_MODEL)),
        _const_spec((D_MODEL, D_IN)),
    ]
    out_shape = [jax.ShapeDtypeStruct((n, D_IN), out_dtype)]
    out_specs = [pl.BlockSpec((tile, D_IN), lambda i: (i, 0))]
    kw = dict(win_first_tile=0, tiles_per_seq=1, n_win=0, n_perm=len(perm_dilations))
    scratch = []
    if win_rows is not None:
        assert seq_len % tile == 0 and win_rows % tile == 0
        tps = seq_len // tile
        first = (seq_len - win_rows) // tile
        kw.update(win_first_tile=first, tiles_per_seq=tps, n_win=2)

        def win_map(i):
            return (i // tps, 0, jnp.maximum(i % tps - first, 0))

        for _ in range(2):
            out_shape.append(jax.ShapeDtypeStruct((n // seq_len, D_HEADS, win_rows), F32))
            out_specs.append(pl.BlockSpec((1, D_HEADS, tile), win_map))
    if perm_dilations:
        tps = seq_len // tile
        sub = min(tile, INPROJ_SUB_ROWS)
        for level, d in enumerate(perm_dilations, start=1):
            assert d == DIL_STEP ** level and sub % (16 * d) == 0
            out_shape.append(jax.ShapeDtypeStruct((n // seq_len, d, seq_len // d, 3 * D_HEADS), BF16))
            out_specs.append(pl.BlockSpec((1, d, tile // d, 3 * D_HEADS), lambda i: (i // tps, 0, i % tps, 0)))
        scratch.append(pltpu.VMEM((len(perm_dilations), tile // sub, 3 * (D_HEADS // PAIR), sub, PAIR), F32))
    return pl.pallas_call(
        functools.partial(_inproj_kernel, **kw),
        grid=(n // tile,),
        in_specs=in_specs,
        out_specs=out_specs,
        out_shape=out_shape,
        scratch_shapes=scratch,
        compiler_params=_params(("arbitrary",)),
        name="inproj",
    )(x2d, norm_w.reshape(1, D_MODEL), w_bf16)


def _retention_tables(chunk):
    lg = np.log1p(-np.exp2(-5.0 - np.arange(N_HEADS, dtype=np.float64)))
    i = np.arange(chunk, dtype=np.float64)
    diff = i[:, None] - i[None, :]
    intra = np.where(diff >= 0, np.exp(lg[:, None, None] * np.maximum(diff, 0.0)), 0.0)
    lane_head = np.arange(D_HEADS) // HEAD_DIM
    q_dec = np.exp(lg[lane_head][None, :] * (i[:, None] + 1.0))
    k_dec = np.exp(lg[lane_head][None, :] * (chunk - 1.0 - i[:, None]))
    c_dec = np.exp(lg * chunk)
    row_head = np.arange(PAIR) // HEAD_DIM
    same = (row_head[:, None] == row_head[None, :]).astype(np.float64)
    c_pair = c_dec.reshape(N_PAIRS, 2)[:, row_head][:, :, None] * same[None]
    return tuple(a.astype(np.float32) for a in (intra, q_dec, k_dec, c_pair, same))


def _retention_init(*refs):
    state_ref = refs[-1]

    @pl.when(pl.program_id(0) == 0)
    def _():
        state_ref[...] = jnp.zeros(state_ref.shape, state_ref.dtype)


def _retention_kernel(q_ref, k_ref, v_ref, g_ref, s0_ref, intra_ref, qdec_ref, kdec_ref, cpair_ref, same_ref,
                      gw_ref, gb_ref, ro_ref, sout_ref, state_ref, *, chunk, chunks_per_tile, tiles_per_seq):
    n_pairs = N_PAIRS
    first_tile = pl.program_id(0) % tiles_per_seq == 0
    left = lax.broadcasted_iota(jnp.int32, (chunk, PAIR), 1) < HEAD_DIM

    averager = (same_ref[...] * (1.0 / HEAD_DIM)).astype(BF16)

    def head_mean(a):
        hi = a.astype(BF16)
        lo = (a - hi.astype(F32)).astype(BF16)
        return _dot(hi, averager) + _dot(lo, averager)

    items = [(c, p) for p in range(n_pairs) for c in range(chunks_per_tile)]
    loaded = {}
    for c, p in items:
        rows = slice(c * chunk, (c + 1) * chunk)
        loaded[c, p] = tuple(ref[0, rows, p * PAIR:(p + 1) * PAIR] for ref in (q_ref, k_ref, v_ref, g_ref))
    kept = [state_ref[p] for p in range(n_pairs)]
    zero_block = jnp.zeros((HEAD_DIM, HEAD_DIM), F32)

    def pair_state(p):
        top = jnp.concatenate([s0_ref[0, 2 * p], zero_block], axis=1)
        bottom = jnp.concatenate([zero_block, s0_ref[0, 2 * p + 1]], axis=1)
        return jnp.concatenate([top, bottom], axis=0)

    states = [jnp.where(first_tile, pair_state(p), kept[p]) for p in range(n_pairs)]
    lanes_of = {p: slice(p * PAIR, (p + 1) * PAIR) for p in range(n_pairs)}
    q2 = {cp: loaded[cp][0].astype(BF16) for cp in items}
    k2 = {cp: loaded[cp][1].astype(BF16) for cp in items}
    v2 = {cp: loaded[cp][2].astype(BF16) for cp in items}
    raw = {}
    for cp in items:
        for hh in range(2):
            mine = left if hh == 0 else jnp.logical_not(left)
            raw[cp, hh] = _dot_nt(jnp.where(mine, q2[cp], jnp.zeros_like(q2[cp])), k2[cp])
    yield
    sc = {(cp, hh): (raw[cp, hh] * intra_ref[2 * cp[1] + hh]).astype(BF16) for cp in items for hh in range(2)}
    kd = {cp: (k2[cp].astype(F32) * kdec_ref[:, lanes_of[cp[1]]]).astype(BF16) for cp in items}
    yield
    intra_out = {cp: jnp.where(left, _dot(sc[cp, 0], v2[cp]), _dot(sc[cp, 1], v2[cp])) for cp in items}
    yield
    kv = {cp: _dot_tn(kd[cp], v2[cp]) * same_ref[...] for cp in items}
    yield
    state_at = {}
    for c, p in items:
        state_at[c, p] = states[p]
        states[p] = states[p] * cpair_ref[p] + kv[c, p]
    o = {cp: intra_out[cp] + _dot(q2[cp], state_at[cp].astype(BF16)) * qdec_ref[:, lanes_of[cp[1]]] for cp in items}
    yield
    mean = {cp: head_mean(o[cp]) for cp in items}
    dlt = {cp: o[cp] - mean[cp] for cp in items}
    var = {cp: head_mean(dlt[cp] * dlt[cp]) for cp in items}
    yield
    outs = {}
    for cp in items:
        lanes = lanes_of[cp[1]]
        y = dlt[cp] * lax.rsqrt(var[cp] + EPS) * gw_ref[:, lanes] + gb_ref[:, lanes]
        outs[cp] = (_silu(loaded[cp][3].astype(F32)) * y).astype(ro_ref.dtype)
    yield
    for c, p in items:
        ro_ref[0, c * chunk:(c + 1) * chunk, lanes_of[p]] = outs[c, p]
    for p in range(n_pairs):
        state_ref[p] = states[p]
        sout_ref[0, 2 * p] = states[p][:HEAD_DIM, :HEAD_DIM]
        sout_ref[0, 2 * p + 1] = states[p][HEAD_DIM:, HEAD_DIM:]


class _CallParts:
    def __init__(self, body, pieces, steps, operands, in_specs, out_specs, out_shape, scratch_shapes=(), init=None):
        self.body, self.pieces, self.init, self.steps, self.operands = body, pieces, init, steps, list(operands)
        self.in_specs, self.out_specs, self.out_shape = list(in_specs), list(out_specs), list(out_shape)
        self.scratch_shapes = list(scratch_shapes)

    @property
    def n_refs(self):
        return len(self.operands), len(self.out_shape), len(self.scratch_shapes)


def _run_parts(parts, name):
    steps = parts[0].steps
    assert all(p.steps == steps for p in parts)

    def fused(*refs):
        pos = 0
        groups = []
        for kind in range(3):
            per_part = []
            for p in parts:
                n = p.n_refs[kind]
                per_part.append(refs[pos:pos + n])
                pos += n
            groups.append(per_part)
        for i, p in enumerate(parts):
            if p.init is not None:
                p.init(*groups[0][i], *groups[1][i], *groups[2][i])
        gens = [p.body(*groups[0][i], *groups[1][i], *groups[2][i]) for i, p in enumerate(parts)]
        slots = sorted(((k + 0.5) / p.pieces, i) for i, p in enumerate(parts) for k in range(p.pieces))
        done = object()
        for _, i in slots:
            next(gens[i], done)
        for g in gens:
            assert next(g, done) is done, "a part yielded more pieces than it declared"

    outs = pl.pallas_call(
        fused,
        grid=(steps,),
        in_specs=[s for p in parts for s in p.in_specs],
        out_specs=[s for p in parts for s in p.out_specs],
        out_shape=[s for p in parts for s in p.out_shape],
        scratch_shapes=[s for p in parts for s in p.scratch_shapes],
        compiler_params=_params(("arbitrary",)),
        name=name,
    )(*[o for p in parts for o in p.operands])
    split, pos = [], 0
    for p in parts:
        split.append(outs[pos:pos + len(p.out_shape)])
        pos += len(p.out_shape)
    return split


def _retention_parts(proj, s0, gn_w, gb_b, *, tile):
    b, t, _ = proj.shape
    chunk = math.gcd(t, RET_CHUNK)
    assert t % tile == 0 and tile % chunk == 0
    tps = t // tile

    def grp_spec(g):
        return pl.BlockSpec((1, tile, D_HEADS), lambda i: (i // tps, i % tps, g))

    state_spec = pl.BlockSpec((1, N_HEADS, HEAD_DIM, HEAD_DIM), lambda i: (i // tps, 0, 0, 0))
    return _CallParts(
        functools.partial(_retention_kernel, chunk=chunk, chunks_per_tile=tile // chunk, tiles_per_seq=tps),
        8,
        b * tps,
        (proj, proj, proj, proj, s0.astype(F32), *_retention_tables(chunk),
         gn_w.reshape(1, D_HEADS).astype(F32), gb_b.reshape(1, D_HEADS).astype(F32)),
        in_specs=[
            grp_spec(0), grp_spec(1), grp_spec(2), grp_spec(3),
            state_spec,
            _const_spec((N_HEADS, chunk, chunk)),
            _const_spec((chunk, D_HEADS)),
            _const_spec((chunk, D_HEADS)),
            _const_spec((N_PAIRS, PAIR, PAIR)),
            _const_spec((PAIR, PAIR)),
            _const_spec((1, D_HEADS)),
            _const_spec((1, D_HEADS)),
        ],
        out_specs=[
            pl.BlockSpec((1, tile, D_HEADS), lambda i: (i // tps, i % tps, 0)),
            state_spec,
        ],
        out_shape=[
            jax.ShapeDtypeStruct((b, t, D_HEADS), BF16),
            jax.ShapeDtypeStruct((b, N_HEADS, HEAD_DIM, HEAD_DIM), F32),
        ],
        scratch_shapes=[pltpu.VMEM((N_PAIRS, PAIR, PAIR), F32)],
        init=_retention_init,
    )


def _alibi_slopes():
    return np.exp2(-8.0 * np.arange(1, N_HEADS + 1, dtype=np.float64) / N_HEADS)


def _prompt_attn_kernel(*refs, seq):
    n_br = len(BRANCHES)
    branch_refs = [refs[3 * bi:3 * bi + 3] for bi in range(n_br)]
    slope_ref, o_ref, st_ref = refs[3 * n_br:]
    hp = pl.program_id(1)
    left = lax.broadcasted_iota(jnp.int32, (Q_TILE, PAIR), 1) < HEAD_DIM
    ones = jnp.ones((2 * Q_TILE, PAIR), BF16)

    def run_group(tiles, biases, state_idx, first_branch, last_branch, qs, ks, vs):
        ops = [(qs[pl.ds(q0, Q_TILE), :], ks[pl.ds(k0, nk), :], vs[pl.ds(k0, nk), :]) for q0, k0, nk in tiles]
        old = None if first_branch else [tuple(st_ref[c, idx, :] for c in range(3)) for idx in state_idx]
        scores = []
        for (q, k, _), bias in zip(ops, biases):
            for hh in range(2):
                mine = left if hh == 0 else jnp.logical_not(left)
                scores.append(_dot_nt(jnp.where(mine, q, jnp.zeros_like(q)), k) + bias[hh])
        probs = []
        for s in scores:
            m = jnp.max(s, axis=-1, keepdims=True)
            probs.append((m, jnp.exp(s - m).astype(BF16)))
        new = []
        for ti, (_, _, v) in enumerate(ops):
            vaug = jnp.concatenate([v, ones[:v.shape[0]]], axis=1)
            res = []
            for hh in range(2):
                m, p = probs[2 * ti + hh]
                pv = _dot(p, vaug)
                res.append((jnp.broadcast_to(m, (Q_TILE, PAIR)), pv[:, PAIR:], pv[:, :PAIR]))
            m, den, num = (jnp.where(left, a, b) for a, b in zip(res[0], res[1]))
            if first_branch:
                new.append((m, den, num))
            else:
                m0, den0, num0 = old[ti]
                mm = jnp.maximum(m0, m)
                w0 = jnp.exp(m0 - mm)
                w1 = jnp.exp(m - mm)
                den1, num1 = den0 * w0 + den * w1, num0 * w0 + num * w1
                new.append((num1 / den1,) if last_branch else (mm, den1, num1))
        for idx, vals in zip(state_idx, new):
            for c, val in enumerate(vals):
                st_ref[c, idx, :] = val

    assert n_br >= 2 and all(BRANCHES[bi][1] == DIL_STEP ** bi for bi in range(n_br))
    for step, bi in enumerate(reversed(range(n_br))):
        window, d = BRANCHES[bi]
        assert window // d == Q_TILE
        seg = seq // d
        tiles_per_seg = seg // Q_TILE
        segs_per_iter = max(1, ATTN_GROUPS[bi] // tiles_per_seg)
        group = min(ATTN_GROUPS[bi], tiles_per_seg)
        assert tiles_per_seg % group == 0
        qi = lax.broadcasted_iota(jnp.int32, (Q_TILE, 2 * Q_TILE), 0)
        kj = lax.broadcasted_iota(jnp.int32, (Q_TILE, 2 * Q_TILE), 1)
        n_rest = qi + Q_TILE - kj
        n_first = n_rest[:, :Q_TILE] - Q_TILE
        bias_rest, bias_first = [], []
        for hh in range(2):
            sl = slope_ref[pl.ds(2 * hp + hh, 1), :]
            sl2 = jnp.concatenate([sl, sl], axis=1)
            pen_rest = sl2 * (d * n_rest).astype(F32)
            pen_first = sl * (d * n_first).astype(F32)
            bias_rest.append(jnp.where((n_rest >= 0) & (n_rest <= Q_TILE), -pen_rest, NEG_INF))
            bias_first.append(jnp.where(n_first >= 0, -pen_first, NEG_INF))

        qs, ks, vs = (ref.at[0] for ref in branch_refs[bi])

        def seg_body(r0, carry, *, d=d, seg=seg, tiles_per_seg=tiles_per_seg, group=group,
                     segs_per_iter=segs_per_iter, qs=qs, ks=ks, vs=vs,
                     bias_first=bias_first, bias_rest=bias_rest, first_branch=(step == 0),
                     last_branch=(step == n_br - 1)):
            def residue(s, dil):
                if dil == 1:
                    return 0
                return (dil // DIL_STEP) * (s % DIL_STEP) + residue(s // DIL_STEP, dil // DIL_STEP)

            def tile_spec(s, tile_in_seg, is_first):
                q0 = pl.multiple_of(s * seg + tile_in_seg * Q_TILE, Q_TILE)
                start = d * Q_TILE * tile_in_seg + residue(s, d)
                rows = pl.ds(start, Q_TILE) if d == 1 else pl.ds(start, Q_TILE, stride=d)
                if is_first:
                    return (q0, q0, Q_TILE), bias_first, rows
                return (q0, q0 - Q_TILE, 2 * Q_TILE), bias_rest, rows

            def run(specs):
                tiles, biases, rows = zip(*specs)
                run_group(tiles, biases, rows, first_branch, last_branch, qs, ks, vs)

            segs = [r0 * segs_per_iter + i for i in range(segs_per_iter)]
            run([tile_spec(r, u, u == 0) for r in segs for u in range(group)])

            def group_body(g, carry2):
                run([tile_spec(segs[0], g * group + u, False) for u in range(group)])
                return carry2

            if tiles_per_seg > group:
                carry = lax.fori_loop(1, tiles_per_seg // group, group_body, carry)
            return carry

        lax.fori_loop(0, d // segs_per_iter, seg_body, 0)

    assert len(BRANCHES) >= 2
    o_ref[0] = st_ref[0].astype(o_ref.dtype)


def _prompt_attn(proj, perms):
    b, t, _ = proj.shape
    assert t % (BRANCHES[-1][1] * Q_TILE) == 0 and len(perms) == len(BRANCHES) - 1
    slopes = np.broadcast_to(_alibi_slopes()[:, None], (N_HEADS, PAIR)).astype(np.float32)

    def col_spec(group):
        return pl.BlockSpec((1, t, PAIR), lambda i, j: (i, 0, group * N_PAIRS + j))

    return pl.pallas_call(
        functools.partial(_prompt_attn_kernel, seq=t),
        grid=(b, N_PAIRS),
        in_specs=[col_spec(4), col_spec(5), col_spec(6)] + [col_spec(a) for _ in perms for a in range(3)]
        + [_const_spec((N_HEADS, PAIR))],
        out_specs=pl.BlockSpec((1, t, PAIR), lambda i, j: (i, 0, j)),
        out_shape=jax.ShapeDtypeStruct((b, t, D_HEADS), BF16),
        scratch_shapes=[pltpu.VMEM((3, t, PAIR), F32)],
        compiler_params=_params(("arbitrary", "arbitrary")),
        name="prompt_attn",
    )(proj, proj, proj, *[p.reshape(b, t, 3 * D_HEADS) for p in perms for _ in range(3)], slopes)


def _sample_tables(n_new, n_cache):
    slopes = _alibi_slopes()
    qpos = n_cache + np.arange(n_new)
    kpos = np.arange(n_cache + n_new)
    dist = qpos[:, None] - kpos[None, :]
    cnt = np.zeros(dist.shape, np.float32)
    for window, d in BRANCHES:
        cnt = cnt + ((dist >= 0) & (dist <= window) & (dist % d == 0)).astype(np.float32)
    bias = -slopes[:, None, None] * dist.astype(np.float64)[None]
    bias = np.where(cnt[None] > 0, bias, NEG_INF).reshape(N_HEADS * n_new, -1).astype(np.float32)
    return bias[:, :n_cache], cnt[:, :n_cache], bias[:, n_cache:], cnt[:, n_cache:]


def _sample_attn_kernel(q_ref, kn_ref, vn_ref, knt_ref, vnt_ref, ck_ref, cv_ref, bc_ref, cc_ref, bn_ref, cn_ref,
                        o_ref, wk_ref, wv_ref, *, n_new, n_cache):
    keep = n_cache - n_new
    heads = range(ck_ref.shape[1])
    lanes = [slice(h * HEAD_DIM, (h + 1) * HEAD_DIM) for h in heads]
    rows = [slice(h * n_new, (h + 1) * n_new) for h in heads]
    q = [q_ref[0, :, lanes[h]].astype(BF16) for h in heads]
    k_new = [kn_ref[0, :, lanes[h]].astype(BF16) for h in heads]
    v_new = [vn_ref[0, :, lanes[h]].astype(BF16) for h in heads]
    s_c = [_dot(q[h], ck_ref[0, h].astype(BF16)) + bc_ref[rows[h], :] for h in heads]
    s_n = [_dot_nt(q[h], k_new[h]) + bn_ref[rows[h], :] for h in heads]
    yield
    m = [jnp.maximum(jnp.max(s_c[h], axis=-1, keepdims=True), jnp.max(s_n[h], axis=-1, keepdims=True))
         for h in heads]
    p_c = [cc_ref[...] * jnp.exp(s_c[h] - m[h]) for h in heads]
    p_n = [cn_ref[...] * jnp.exp(s_n[h] - m[h]) for h in heads]
    den = [jnp.sum(p_c[h], axis=-1, keepdims=True) + jnp.sum(p_n[h], axis=-1, keepdims=True) for h in heads]
    yield
    o = [_dot_nt(p_c[h].astype(BF16), cv_ref[0, h].astype(BF16)) + _dot(p_n[h].astype(BF16), v_new[h])
         for h in heads]
    yield
    for h in heads:
        wk_ref[0, h] = pltpu.roll(ck_ref[0, h], keep, 1)
        wv_ref[0, h] = pltpu.roll(cv_ref[0, h], keep, 1)
        wk_ref[0, h, :, keep:n_cache] = knt_ref[0, lanes[h], :]
        wv_ref[0, h, :, keep:n_cache] = vnt_ref[0, lanes[h], :]
        o_ref[0, :, lanes[h]] = (o[h] / den[h]).astype(o_ref.dtype)
        yield


def _sample_attn_parts(proj, knt, vnt, cache_kt, cache_vt):
    b, n_new, _ = proj.shape
    n_cache = cache_kt.shape[-1]
    assert n_cache == MAX_WINDOW and n_new % 8 == 0
    rows = N_HEADS * n_new
    tables = _sample_tables(n_new, n_cache)

    def grp_spec(group):
        return pl.BlockSpec((1, n_new, D_HEADS), lambda i: (i, 0, group))

    new_t_spec = pl.BlockSpec((1, D_HEADS, n_new), lambda i: (i, 0, 0))
    cache_spec = pl.BlockSpec((1, N_HEADS, HEAD_DIM, n_cache), lambda i: (i, 0, 0, 0))
    return _CallParts(
        functools.partial(_sample_attn_kernel, n_new=n_new, n_cache=n_cache),
        N_HEADS + 4,
        b,
        (proj, proj, proj, knt, vnt, cache_kt, cache_vt, *tables),
        in_specs=[grp_spec(4), grp_spec(5), grp_spec(6), new_t_spec, new_t_spec, cache_spec, cache_spec,
                  _const_spec((rows, n_cache)), _const_spec((n_new, n_cache)),
                  _const_spec((rows, n_new)), _const_spec((n_new, n_new))],
        out_specs=[pl.BlockSpec((1, n_new, D_HEADS), lambda i: (i, 0, 0)), cache_spec, cache_spec],
        out_shape=[
            jax.ShapeDtypeStruct((b, n_new, D_HEADS), BF16),
            jax.ShapeDtypeStruct((b, N_HEADS, HEAD_DIM, n_cache), F32),
            jax.ShapeDtypeStruct((b, N_HEADS, HEAD_DIM, n_cache), F32),
        ],
    )


FF_CHUNK = 1408
N_FF_CHUNKS = D_FF // FF_CHUNK
HALO = 8
FFN_SUB_ROWS = 512


def _mix_ffn_kernel(x_ref, ro_ref, ao_ref, wo_ref, n2_ref, wup_ref, cw_ref, cb_ref, wdn_ref, nf_ref, prev_ref,
                    y_ref, ua_ref, halo_ref, *, tile, seq_in_tile, tiles_per_seq):
    t = pl.program_id(0) % tiles_per_seq
    sub = min(tile, FFN_SUB_ROWS)
    chunk_cols = [slice(c * FF_CHUNK, (c + 1) * FF_CHUNK) for c in range(N_FF_CHUNKS)]
    if seq_in_tile is None:
        halo = jnp.where(t == 0, prev_ref[0], halo_ref[...])
        halos = [halo[:, cols] for cols in chunk_cols]
        head_row = lax.broadcasted_iota(jnp.int32, (HALO, FF_CHUNK), 0)
    else:
        assert sub == tile
        pos = lax.broadcasted_iota(jnp.int32, (tile, FF_CHUNK), 0) % seq_in_tile
    ys, uas = [], None
    for i in range(tile // sub):
        rows = slice(i * sub, (i + 1) * sub)
        x1 = (x_ref[0, rows, :] + _dot(ro_ref[0, rows, :], wo_ref[0:D_HEADS, :])
              + _dot(ao_ref[0, rows, :], wo_ref[D_HEADS:2 * D_HEADS, :]))
        h2 = _rmsnorm(x1, n2_ref[...]).astype(BF16)
        yield
        gates, uas = [], []
        for c, cols in enumerate(chunk_cols):
            ua = _dot(h2, wup_ref[:, cols])
            yield
            ub = _dot(h2, wup_ref[:, D_FF + c * FF_CHUNK:D_FF + (c + 1) * FF_CHUNK])
            yield
            conv = cb_ref[:, cols]
            for j in range(CONV_W - 1):
                back = CONV_W - 1 - j
                tap = pltpu.roll(ua, back, 0)
                if seq_in_tile is None:
                    head = jnp.where(head_row < back, pltpu.roll(halos[c], back, 0), tap[0:HALO])
                    tap = jnp.concatenate([head, tap[HALO:]], axis=0)
                else:
                    tap = jnp.where(pos < back, prev_ref[j, :, cols], tap)
                conv = conv + tap * cw_ref[j:j + 1, cols]
            conv = conv + ua * cw_ref[CONV_W - 1:CONV_W, cols]
            gates.append((_silu(conv) * ub).astype(BF16))
            uas.append(ua)
            if seq_in_tile is None:
                halos[c] = ua[sub - HALO:sub]
            yield
        ys.append(_rmsnorm(x1 + _dot(jnp.concatenate(gates, axis=1), wdn_ref[...]), nf_ref[...]))
        yield
    for i, y in enumerate(ys):
        y_ref[0, i * sub:(i + 1) * sub, :] = y
    for c, cols in enumerate(chunk_cols):
        if seq_in_tile is None:
            ua_ref[0, :, cols] = halos[c]
            halo_ref[:, cols] = halos[c]
        else:
            ua_ref[0, :, cols] = uas[c]


def _mix_ffn_parts(x, ro, ao, w_out, norm2_w, w_up, conv_w, conv_b, w_down, normf_w, prev, *, tile, seq_in_tile):
    b, t, _ = x.shape
    assert t % tile == 0
    tps = t // tile
    if seq_in_tile is None:
        prev_spec = pl.BlockSpec((1, HALO, D_FF), lambda i: (i // tps, 0, 0))
        ua_rows = HALO
    else:
        assert b == 1 and t == tile
        prev_spec = _const_spec((CONV_W - 1, tile, D_FF))
        ua_rows = tile

    def row_spec(width):
        return pl.BlockSpec((1, tile, width), lambda i: (i // tps, i % tps, 0))

    return _CallParts(
        functools.partial(_mix_ffn_kernel, tile=tile, seq_in_tile=seq_in_tile, tiles_per_seq=tps),
        (tile // min(tile, FFN_SUB_ROWS)) * (2 + 3 * N_FF_CHUNKS) + 1,
        b * tps,
        (x, ro, ao, w_out, norm2_w.reshape(1, D_MODEL), w_up, conv_w, conv_b.reshape(1, D_FF), w_down,
         normf_w.reshape(1, D_MODEL), prev),
        in_specs=[
            row_spec(D_MODEL),
            row_spec(D_HEADS),
            row_spec(D_HEADS),
            _const_spec((2 * D_HEADS, D_MODEL)),
            _const_spec((1, D_MODEL)),
            _const_spec((D_MODEL, 2 * D_FF)),
            _const_spec((CONV_W, D_FF)),
            _const_spec((1, D_FF)),
            _const_spec((D_FF, D_MODEL)),
            _const_spec((1, D_MODEL)),
            prev_spec,
        ],
        out_specs=[
            row_spec(D_MODEL),
            pl.BlockSpec((1, ua_rows, D_FF), lambda i: (i // tps, 0, 0)),
        ],
        out_shape=[
            jax.ShapeDtypeStruct((b, t, D_MODEL), F32),
            jax.ShapeDtypeStruct((b, ua_rows, D_FF), F32),
        ],
        scratch_shapes=[pltpu.VMEM((HALO, D_FF), F32)],
    )


INPROJ_ROW_TILE = 512
RET_ROW_TILE = 512
FFN_ROW_TILE = 512


def _rows_to_lanes(cache):
    return cache.transpose(0, 2, 3, 1)


def _lanes_to_rows(cache_t):
    return cache_t.transpose(0, 3, 1, 2)


def kernel(x_prompt, x_sample, state_ret, cache_win_k, cache_win_v, state_conv, norm1_w, w_in, ret_gn_w, ret_gn_b,
           w_out, norm2_w, w_up, conv_w, conv_b, w_down, normf_w):
    bp, tp, _ = x_prompt.shape
    bs, ts, _ = x_sample.shape
    w_in_b, w_out_b, w_up_b, w_down_b = (w.astype(BF16) for w in (w_in, w_out, w_up, w_down))
    ffn_w = (w_out_b, norm2_w, w_up_b, conv_w, conv_b, w_down_b, normf_w)

    win = min(MAX_WINDOW, tp)
    proj_p, wk_p, wv_p, *perms_p = _inproj(x_prompt.reshape(bp * tp, D_MODEL), norm1_w, w_in_b, tile=INPROJ_ROW_TILE,
                                           out_dtype=BF16, seq_len=tp, win_rows=win,
                                           perm_dilations=[d for _, d in BRANCHES[1:]])
    proj_p = proj_p.reshape(bp, tp, D_IN)
    proj_s, knt, vnt = _inproj(x_sample.reshape(bs * ts, D_MODEL), norm1_w, w_in_b, tile=bs * ts, out_dtype=F32,
                               seq_len=bs * ts, win_rows=bs * ts)
    proj_s = proj_s.reshape(bs, ts, D_IN)
    knt, vnt = (a.reshape(D_HEADS, bs, ts).transpose(1, 0, 2) for a in (knt, vnt))

    ao_p = _prompt_attn(proj_p, perms_p)

    ret_p_parts = _retention_parts(proj_p, jnp.zeros((bp, N_HEADS, HEAD_DIM, HEAD_DIM), F32), ret_gn_w, ret_gn_b,
                                   tile=RET_ROW_TILE)
    smp_parts = _sample_attn_parts(proj_s, knt, vnt, _rows_to_lanes(cache_win_k), _rows_to_lanes(cache_win_v))
    ret_s_parts = _retention_parts(proj_s, state_ret, ret_gn_w, ret_gn_b, tile=ts)
    if ret_p_parts.steps == smp_parts.steps == ret_s_parts.steps:
        (ro_p, ret_p), (ao_s, wk_s, wv_s), (ro_s, ret_s) = _run_parts(
            [ret_p_parts, smp_parts, ret_s_parts], "retention_and_cache")
    else:
        (ro_p, ret_p), = _run_parts([ret_p_parts], "retention")
        (ao_s, wk_s, wv_s), = _run_parts([smp_parts], "sample_attn")
        (ro_s, ret_s), = _run_parts([ret_s_parts], "retention")
    (y_p, ua_tail_p), = _run_parts(
        [_mix_ffn_parts(x_prompt, ro_p, ao_p, *ffn_w, jnp.zeros((bp, HALO, D_FF), F32), tile=FFN_ROW_TILE,
                        seq_in_tile=None)], "mix_ffn")
    conv_p = ua_tail_p[:, HALO - (CONV_W - 1):]

    pad = jnp.zeros((bs, ts - (CONV_W - 1), D_FF), F32)
    prev_s = jnp.stack([
        jnp.concatenate([state_conv[:, j:], pad, jnp.zeros((bs, j, D_FF), F32)], axis=1).reshape(bs * ts, D_FF)
        for j in range(CONV_W - 1)])
    (y_s, ua_s), = _run_parts(
        [_mix_ffn_parts(x_sample.reshape(1, bs * ts, D_MODEL), ro_s.reshape(1, bs * ts, D_HEADS),
                        ao_s.reshape(1, bs * ts, D_HEADS), *ffn_w, prev_s, tile=bs * ts, seq_in_tile=ts)],
        "mix_ffn")
    conv_s = ua_s.reshape(bs, ts, D_FF)[:, ts - (CONV_W - 1):]

    return (y_p, y_s.reshape(bs, ts, D_MODEL),
            ret_p.astype(x_prompt.dtype), ret_s.astype(state_ret.dtype),
            _lanes_to_rows(wk_p.reshape(bp, N_HEADS, HEAD_DIM, win)),
            _lanes_to_rows(wv_p.reshape(bp, N_HEADS, HEAD_DIM, win)),
            _lanes_to_rows(wk_s), _lanes_to_rows(wv_s),
            conv_p, conv_s)
```
